```python
import jax, jax.numpy as jnp
from jax import lax
import numpy as np

D_MODEL = 2048
BATCH = 8
SEQ = 2048
DEPTH = 2

N_MIXERS = 2
N_HEADS = 16
HEAD_DIM = D_MODEL // N_HEADS
ATTN_WIDTH = N_HEADS * HEAD_DIM
MOBA_BLOCK = 256
MOBA_TOPK = 3
Q_CHUNK = 8
ROPE_THETA = 10000.0
CONV_WIDTH = D_MODEL
CONV_K = 31
RMS_EPS = 1e-6
LN_EPS = 1e-5
NEG = -1e30
N_ATTN_LAYERS = (DEPTH + 1) // 2
N_CONV_LAYERS = DEPTH // 2

kernel_name = "hybrid_moba_conformer_sandwich"


def rms_norm(x, g):
    xf = x.astype(jnp.float32)
    y = xf * lax.rsqrt(jnp.mean(xf * xf, axis=-1, keepdims=True) + RMS_EPS)
    return (y * g.astype(jnp.float32)).astype(x.dtype)


def layer_norm(x, g, b):
    xf = x.astype(jnp.float32)
    mu = jnp.mean(xf, axis=-1, keepdims=True)
    var = jnp.mean(jnp.square(xf - mu), axis=-1, keepdims=True)
    y = (xf - mu) * lax.rsqrt(var + LN_EPS)
    return (y * g.astype(jnp.float32) + b.astype(jnp.float32)).astype(x.dtype)


def apply_rope(t):
    s, d = t.shape[2], t.shape[3]
    half = d // 2
    inv_freq = 1.0 / (ROPE_THETA ** (jnp.arange(half, dtype=jnp.float32) * 2.0 / d))
    ang = jnp.arange(s, dtype=jnp.float32)[:, None] * inv_freq[None, :]
    cos, sin = jnp.cos(ang), jnp.sin(ang)
    tf = t.astype(jnp.float32)
    t1, t2 = tf[..., :half], tf[..., half:]
    return jnp.concatenate([t1 * cos - t2 * sin, t2 * cos + t1 * sin], axis=-1).astype(t.dtype)


def moba_attention(q, k, v):
    b, h, s, d = q.shape
    nb = -(-s // MOBA_BLOCK)
    pad = nb * MOBA_BLOCK - s
    kp = jnp.pad(k, ((0, 0), (0, 0), (0, pad), (0, 0)))
    vp = jnp.pad(v, ((0, 0), (0, 0), (0, pad), (0, 0)))
    k_blk = kp.reshape(b, h, nb, MOBA_BLOCK, d)
    v_blk = vp.reshape(b, h, nb, MOBA_BLOCK, d)
    k_mean = jnp.mean(k_blk.astype(jnp.float32), axis=3)
    scale = d ** -0.5

    q_blk = jnp.arange(s) // MOBA_BLOCK
    gate = jnp.einsum('bhsd,bhnd->bhsn', q.astype(jnp.float32), k_mean)
    past = jnp.arange(nb)[None, :] < q_blk[:, None]
    gate = jnp.where(past, gate, NEG)
    kk = min(MOBA_TOPK, nb)
    _, sel = lax.top_k(gate, kk)
    sel_valid = sel < q_blk[:, None]

    b_idx = jnp.arange(b)[:, None, None, None]
    h_idx = jnp.arange(h)[None, :, None, None]
    n_chunks = s // Q_CHUNK

    def chunk(c):
        start = c * Q_CHUNK
        qc = lax.dynamic_slice_in_dim(q, start, Q_CHUNK, axis=2)
        selc = lax.dynamic_slice_in_dim(sel, start, Q_CHUNK, axis=2)
        validc = lax.dynamic_slice_in_dim(sel_valid, start, Q_CHUNK, axis=2)
        j = start // MOBA_BLOCK
        k_own = lax.dynamic_index_in_dim(k_blk, j, axis=2, keepdims=False)
        v_own = lax.dynamic_index_in_dim(v_blk, j, axis=2, keepdims=False)
        k_sel = k_blk[b_idx, h_idx, selc]
        v_sel = v_blk[b_idx, h_idx, selc]
        s_own = jnp.einsum('bhqd,bhkd->bhqk', qc, k_own).astype(jnp.float32) * scale
        s_sel = jnp.einsum('bhqd,bhqnkd->bhqnk', qc, k_sel).astype(jnp.float32) * scale
        qpos = start + jnp.arange(Q_CHUNK)
        kpos = j * MOBA_BLOCK + jnp.arange(MOBA_BLOCK)
        s_own = jnp.where(kpos[None, :] <= qpos[:, None], s_own, NEG)
        s_sel = jnp.where(validc[..., None], s_sel, NEG)
        scores = jnp.concatenate(
            [s_own, s_sel.reshape(b, h, Q_CHUNK, kk * MOBA_BLOCK)], axis=-1)
        p = jax.nn.softmax(scores, axis=-1).astype(v.dtype)
        p_own = p[..., :MOBA_BLOCK]
        p_sel = p[..., MOBA_BLOCK:].reshape(b, h, Q_CHUNK, kk, MOBA_BLOCK)
        return (jnp.einsum('bhqk,bhkd->bhqd', p_own, v_own)
                + jnp.einsum('bhqnk,bhqnkd->bhqd', p_sel, v_sel))

    outs = lax.map(chunk, jnp.arange(n_chunks))
    return outs.transpose(1, 2, 0, 3, 4).reshape(b, h, s, d)


def moba_mixer(hn, w_in, w_out):
    b, s, _ = hn.shape
    proj = hn @ w_in
    q, k, v, z = jnp.split(proj, 4, axis=-1)
    to_heads = lambda t: t.reshape(b, s, N_HEADS, HEAD_DIM).transpose(0, 2, 1, 3)
    q, k, v = apply_rope(to_heads(q)), apply_rope(to_heads(k)), to_heads(v)
    o = moba_attention(q, k, v).transpose(0, 2, 1, 3).reshape(b, s, ATTN_WIDTH)
    return (o * jax.nn.silu(z)) @ w_out


def conformer_mixer(hn, w_in, b_in, w_dw, b_dw, ln_g, ln_b, w_out, b_out):
    proj = hn @ w_in
    ab = proj[..., :2 * CONV_WIDTH] + b_in
    z = proj[..., 2 * CONV_WIDTH:]
    a, g = jnp.split(ab, 2, axis=-1)
    u = a * jax.nn.sigmoid(g)
    u = lax.conv_general_dilated(
        u, w_dw[:, None, :].astype(u.dtype), window_strides=(1,),
        padding=((CONV_K - 1, 0),), dimension_numbers=('NWC', 'WIO', 'NWC'),
        feature_group_count=CONV_WIDTH) + b_dw
    u = jax.nn.silu(layer_norm(u, ln_g, ln_b))
    return (u * jax.nn.silu(z)) @ w_out + b_out


def setup_inputs(seed: int = 0) -> dict:
    key = jax.random.key(seed)
    ks = jax.random.split(key, 16)
    f32 = jnp.float32
    nrm = lambda k, shape, scale: jax.random.normal(k, shape, f32) * scale
    return {
        "x": nrm(ks[0], (BATCH, SEQ, D_MODEL), 1.0),
        "pre_norm_g": 1.0 + nrm(ks[1], (DEPTH, D_MODEL), 0.05),
        "post_norm_g": 1.0 + nrm(ks[2], (DEPTH, D_MODEL), 0.05),
        "attn_w_in": nrm(ks[3], (N_ATTN_LAYERS, D_MODEL, 4 * ATTN_WIDTH), D_MODEL ** -0.5),
        "attn_w_out": nrm(ks[4], (N_ATTN_LAYERS, ATTN_WIDTH, D_MODEL), ATTN_WIDTH ** -0.5),
        "conv_w_in": nrm(ks[5], (N_CONV_LAYERS, D_MODEL, 3 * CONV_WIDTH), D_MODEL ** -0.5),
        "conv_b_in": nrm(ks[6], (N_CONV_LAYERS, 2 * CONV_WIDTH), 0.02),
        "conv_w_dw": nrm(ks[7], (N_CONV_LAYERS, CONV_K, CONV_WIDTH), CONV_K ** -0.5),
        "conv_b_dw": nrm(ks[8], (N_CONV_LAYERS, CONV_WIDTH), 0.02),
        "conv_ln_g": 1.0 + nrm(ks[9], (N_CONV_LAYERS, CONV_WIDTH), 0.05),
        "conv_ln_b": nrm(ks[10], (N_CONV_LAYERS, CONV_WIDTH), 0.02),
        "conv_w_out": nrm(ks[11], (N_CONV_LAYERS, CONV_WIDTH, D_MODEL), CONV_WIDTH ** -0.5),
        "conv_b_out": nrm(ks[12], (N_CONV_LAYERS, D_MODEL), 0.02),
    }


def reference(x, pre_norm_g, post_norm_g, attn_w_in, attn_w_out, conv_w_in, conv_b_in,
              conv_w_dw, conv_b_dw, conv_ln_g, conv_ln_b, conv_w_out, conv_b_out):
    for i in range(DEPTH):
        hn = rms_norm(x, pre_norm_g[i])
        li = i // N_MIXERS
        if i % N_MIXERS == 0:
            y = moba_mixer(hn, attn_w_in[li], attn_w_out[li])
        else:
            y = conformer_mixer(hn, conv_w_in[li], conv_b_in[li], conv_w_dw[li], conv_b_dw[li],
                                conv_ln_g[li], conv_ln_b[li], conv_w_out[li], conv_b_out[li])
        x = x + rms_norm(y, post_norm_g[i])
    return x
```

```python
import functools

import jax
import jax.numpy as jnp
from jax import lax
from jax.experimental import pallas as pl
from jax.experimental.pallas import tpu as pltpu

F32 = jnp.float32
BF16 = jnp.bfloat16

N_MIXERS = 2
N_HEADS = 16
MOBA_BLOCK = 256
MOBA_TOPK = 3
ROPE_THETA = 10000.0
CONV_K = 31
RMS_EPS = 1e-6
LN_EPS = 1e-5
NEG = -1e30

LANES = 128
HALO_ROWS = 32
VMEM_LIMIT_BYTES = 56 * 1024 * 1024

_NT = (((1,), (1,)), ((), ()))


def _params(n_axes):
    return pltpu.CompilerParams(
        dimension_semantics=("arbitrary",) * n_axes,
        vmem_limit_bytes=VMEM_LIMIT_BYTES)


def _const_spec(shape):
    return pl.BlockSpec(shape, lambda *_: (0,) * len(shape), pipeline_mode=pl.Buffered(1))


def _rms_norm_rows(x_ref, g_ref, hn_ref, row_chunk):
    def body(r, carry):
        r0 = pl.multiple_of(r * row_chunk, row_chunk)
        xf = x_ref[pl.ds(r0, row_chunk), :]
        ms = jnp.mean(xf * xf, axis=-1, keepdims=True)
        hn_ref[pl.ds(r0, row_chunk), :] = (xf * lax.rsqrt(ms + RMS_EPS) * g_ref[...]).astype(BF16)
        return carry
    lax.fori_loop(0, x_ref.shape[0] // row_chunk, body, 0)


def _silu(t):
    return t * jax.nn.sigmoid(t)


def _attn_in_kernel(x_ref, g_ref, w_ref, tab_ref, o_ref, hn_ref, *, rope_tiles):
    j = pl.program_id(1)

    @pl.when(j == 0)
    def _():
        _rms_norm_rows(x_ref, g_ref, hn_ref, 64)

    acc = jnp.dot(hn_ref[...], w_ref[...], preferred_element_type=F32)

    @pl.when(j < rope_tiles)
    def _():
        cos = tab_ref[0, 0]
        sin = tab_ref[0, 1]
        for h in range(acc.shape[1] // LANES):
            c = acc[:, h * LANES:(h + 1) * LANES]
            rot = pltpu.roll(c, LANES // 2, 1)
            o_ref[:, h * LANES:(h + 1) * LANES] = (c * cos + rot * sin).astype(BF16)

    @pl.when(j >= rope_tiles)
    def _():
        o_ref[...] = acc.astype(BF16)


def _attn_in(x2, g, w, tab, seq, tm=1024, tn=1024):
    t, d = x2.shape
    n = w.shape[1]
    head_dim = tab.shape[-1]
    qk_tiles = (n // 4) // tn
    tiles_per_seq = seq // tm
    return pl.pallas_call(
        functools.partial(_attn_in_kernel, rope_tiles=2 * qk_tiles),
        grid=(t // tm, n // tn),
        in_specs=[
            pl.BlockSpec((tm, d), lambda i, j: (i, 0)),
            pl.BlockSpec((1, d), lambda i, j: (0, 0)),
            pl.BlockSpec((d, tn), lambda i, j: (0, j)),
            pl.BlockSpec((1, 2, tm, head_dim),
                         lambda i, j: (jnp.minimum(j // qk_tiles, 1), 0, i % tiles_per_seq, 0)),
        ],
        out_specs=pl.BlockSpec((tm, tn), lambda i, j: (i, j)),
        out_shape=jax.ShapeDtypeStruct((t, n), BF16),
        scratch_shapes=[pltpu.VMEM((tm, d), BF16)],
        compiler_params=_params(2),
        name="attn_in",
    )(x2, g, w, tab)


def _moba_kernel(q_ref, k_ref, v_ref, o_ref, vt_ref):
    s_len, hd = q_ref.shape
    nb = s_len // MOBA_BLOCK
    blk = MOBA_BLOCK
    ext = vt_ref.shape[0]

    for n in range(nb):
        vb = v_ref[n * blk:(n + 1) * blk, :].astype(F32)
        vt_ref[0:hd, n * blk:(n + 1) * blk] = vb.T.astype(BF16)
    vt_ref[hd:ext, :] = jnp.ones((ext - hd, s_len), BF16)

    km = jnp.concatenate(
        [jnp.sum(k_ref[n * blk:(n + 1) * blk, :].astype(F32), axis=0, keepdims=True)
         for n in range(nb)], axis=0) * (1.0 / blk)
    km_hi = km.astype(BF16)
    km_lo = (km - km_hi.astype(F32)).astype(BF16)
    q_all = q_ref[...]
    gate = (lax.dot_general(km_hi, q_all, _NT, preferred_element_type=F32)
            + lax.dot_general(km_lo, q_all, _NT, preferred_element_type=F32))

    key_pos = lax.broadcasted_iota(jnp.int32, (blk, blk), 0)
    qry_pos = lax.broadcasted_iota(jnp.int32, (blk, blk), 1)
    causal = key_pos <= qry_pos
    blk_id = lax.broadcasted_iota(jnp.int32, (nb, blk), 0)

    for j in range(nb):
        q_j = q_ref[j * blk:(j + 1) * blk, :]
        g_j = gate[:, j * blk:(j + 1) * blk]
        beaten = jnp.zeros((nb, blk), jnp.int32)
        for m in range(j):
            g_m = g_j[m:m + 1, :]
            beats = (g_m > g_j) | ((g_m == g_j) & (m < blk_id))
            beaten = beaten + beats.astype(jnp.int32)
        bias = jnp.where((beaten < MOBA_TOPK) & (blk_id < j), 0.0, NEG)

        s_all = lax.dot_general(k_ref[0:(j + 1) * blk, :], q_j, _NT,
                                preferred_element_type=F32)
        s_blocks = []
        m_run = None
        for n in range(j + 1):
            s_n = s_all[n * blk:(n + 1) * blk, :]
            if n == j:
                s_n = jnp.where(causal, s_n, NEG)
                m_n = jnp.max(s_n, axis=0, keepdims=True)
            else:
                m_n = jnp.max(s_n, axis=0, keepdims=True) + bias[n:n + 1, :]
            s_blocks.append(s_n)
            m_run = m_n if m_run is None else jnp.maximum(m_run, m_n)
        p_blocks = []
        for n in range(j + 1):
            shift = m_run if n == j else m_run - bias[n:n + 1, :]
            p_blocks.append(jnp.exp(s_blocks[n] - shift).astype(BF16))
        p_all = p_blocks[0] if j == 0 else jnp.concatenate(p_blocks, axis=0)
        acc = jnp.dot(vt_ref[:, 0:(j + 1) * blk], p_all, preferred_element_type=F32)
        out_t = acc[0:hd, :] * (1.0 / acc[hd:hd + 1, :])
        o_ref[j * blk:(j + 1) * blk, :] = out_t.T.astype(BF16)


def _moba(proj, batch, seq, width):
    hd = width // N_HEADS
    t = proj.shape[0]
    ext = hd + 16
    return pl.pallas_call(
        _moba_kernel,
        grid=(batch, N_HEADS),
        in_specs=[
            pl.BlockSpec((seq, hd), lambda b, h: (b, h)),
            pl.BlockSpec((seq, hd), lambda b, h: (b, N_HEADS + h)),
            pl.BlockSpec((seq, hd), lambda b, h: (b, 2 * N_HEADS + h)),
        ],
        out_specs=pl.BlockSpec((seq, hd), lambda b, h: (b, h)),
        out_shape=jax.ShapeDtypeStruct((t, width), BF16),
        scratch_shapes=[pltpu.VMEM((ext, seq), BF16)],
        compiler_params=_params(2),
        name="moba",
    )(proj, proj, proj)


def _post_norm_residual(y, x_ref, pg_ref, o_ref):
    ms = jnp.mean(y * y, axis=-1, keepdims=True)
    o_ref[...] = x_ref[...] + y * lax.rsqrt(ms + RMS_EPS) * pg_ref[...]


def _attn_out_kernel(o_in_ref, z_ref, x_ref, w_ref, pg_ref, o_ref):
    gated = (o_in_ref[...].astype(F32) * _silu(z_ref[...].astype(F32))).astype(BF16)
    y = jnp.dot(gated, w_ref[...], preferred_element_type=F32)
    _post_norm_residual(y, x_ref, pg_ref, o_ref)


def _attn_out(o, proj, x2, w, pg, tm=512):
    t, d = x2.shape
    width = o.shape[1]
    z_block = proj.shape[1] // width - 1
    return pl.pallas_call(
        _attn_out_kernel,
        grid=(t // tm,),
        in_specs=[
            pl.BlockSpec((tm, width), lambda i: (i, 0)),
            pl.BlockSpec((tm, width), lambda i: (i, z_block)),
            pl.BlockSpec((tm, d), lambda i: (i, 0)),
            _const_spec((width, d)),
            _const_spec((1, d)),
        ],
        out_specs=pl.BlockSpec((tm, d), lambda i: (i, 0)),
        out_shape=jax.ShapeDtypeStruct((t, d), F32),
        compiler_params=_params(1),
        name="attn_out",
    )(o, proj, x2, w, pg)


def _conv_in_kernel(x_ref, g_ref, wa_ref, wg_ref, wz_ref, ba_ref, bg_ref, u_ref, sz_ref, hn_ref):
    @pl.when(pl.program_id(1) == 0)
    def _():
        _rms_norm_rows(x_ref, g_ref, hn_ref, 64)

    hn = hn_ref[...]
    a = jnp.dot(hn, wa_ref[...], preferred_element_type=F32) + ba_ref[...]
    g = jnp.dot(hn, wg_ref[...], preferred_element_type=F32) + bg_ref[...]
    u_ref[...] = (a * jax.nn.sigmoid(g)).astype(BF16)
    z = jnp.dot(hn, wz_ref[...], preferred_element_type=F32)
    sz_ref[...] = _silu(z).astype(BF16)


def _conv_in(x2, g, w, b_in, tm=1024, tn=512):
    t, d = x2.shape
    c = w.shape[1] // 3
    nj = c // tn
    return pl.pallas_call(
        _conv_in_kernel,
        grid=(t // tm, nj),
        in_specs=[
            pl.BlockSpec((tm, d), lambda i, j: (i, 0)),
            pl.BlockSpec((1, d), lambda i, j: (0, 0)),
            pl.BlockSpec((d, tn), lambda i, j: (0, j)),
            pl.BlockSpec((d, tn), lambda i, j: (0, nj + j)),
            pl.BlockSpec((d, tn), lambda i, j: (0, 2 * nj + j)),
            pl.BlockSpec((1, tn), lambda i, j: (0, j)),
            pl.BlockSpec((1, tn), lambda i, j: (0, nj + j)),
        ],
        out_specs=[pl.BlockSpec((tm, tn), lambda i, j: (i, j)),
                   pl.BlockSpec((tm, tn), lambda i, j: (i, j))],
        out_shape=[jax.ShapeDtypeStruct((t, c), BF16), jax.ShapeDtypeStruct((t, c), BF16)],
        scratch_shapes=[pltpu.VMEM((tm, d), BF16)],
        compiler_params=_params(2),
        name="conv_in",
    )(x2, g, w, w, w, b_in, b_in)


def _conv_tail_kernel(u_ref, halo_ref, sz_ref, x_ref, wdw_ref, bdw_ref, lng_ref, lnb_ref,
                      w_ref, bo_ref, pg_ref, o_ref, hist_ref, conv_ref, gated_ref,
                      *, tiles_per_seq, row_chunk):
    tm, c = u_ref.shape
    nc = c // LANES

    first = (pl.program_id(0) % tiles_per_seq) == 0
    halo = jnp.where(first, 0.0, halo_ref[...].astype(F32))
    for ci in range(nc):
        hist_ref[ci, 0:HALO_ROWS, :] = halo[:, ci * LANES:(ci + 1) * LANES]
        hist_ref[ci, HALO_ROWS:HALO_ROWS + tm, :] = u_ref[:, ci * LANES:(ci + 1) * LANES].astype(F32)

    first_tap = HALO_ROWS - (CONV_K - 1)

    def lane_chunk(ci, carry):
        w = wdw_ref[ci]
        taps = [jnp.broadcast_to(w[k:k + 1, :], (row_chunk, LANES)) for k in range(CONV_K)]
        b0 = jnp.broadcast_to(bdw_ref[ci], (row_chunk, LANES))
        for r in range(tm // row_chunk):
            acc = b0
            for k in range(CONV_K):
                acc = acc + hist_ref[ci, pl.ds(r * row_chunk + first_tap + k, row_chunk), :] * taps[k]
            conv_ref[ci, pl.ds(r * row_chunk, row_chunk), :] = acc
        return carry
    lax.fori_loop(0, nc, lane_chunk, 0)

    inv_c = 1.0 / c
    s1 = conv_ref[0]
    for ci in range(1, nc):
        s1 = s1 + conv_ref[ci]
    mu = jnp.sum(s1, axis=1, keepdims=True) * inv_c
    s2 = jnp.zeros((tm, LANES), F32)
    for ci in range(nc):
        dlt = conv_ref[ci] - mu
        s2 = s2 + dlt * dlt
    rstd = lax.rsqrt(jnp.sum(s2, axis=1, keepdims=True) * inv_c + LN_EPS)
    for ci in range(nc):
        sl = slice(ci * LANES, (ci + 1) * LANES)
        y = (conv_ref[ci] - mu) * rstd * lng_ref[:, sl] + lnb_ref[:, sl]
        gated_ref[:, sl] = (_silu(y) * sz_ref[:, sl].astype(F32)).astype(BF16)

    y = jnp.dot(gated_ref[...], w_ref[...], preferred_element_type=F32) + bo_ref[...]
    _post_norm_residual(y, x_ref, pg_ref, o_ref)


def _conv_tail(u, sz, x2, wdw3, bdw3, ln_g, ln_b, w, b_out, pg, seq, tm=512, row_chunk=64):
    t, d = x2.shape
    c = u.shape[1]
    nc = c // LANES
    halo_per_tile = tm // HALO_ROWS
    return pl.pallas_call(
        functools.partial(_conv_tail_kernel, tiles_per_seq=seq // tm, row_chunk=row_chunk),
        grid=(t // tm,),
        in_specs=[
            pl.BlockSpec((tm, c), lambda i: (i, 0)),
            pl.BlockSpec((HALO_ROWS, c), lambda i: (jnp.maximum(i * halo_per_tile - 1, 0), 0)),
            pl.BlockSpec((tm, c), lambda i: (i, 0)),
            pl.BlockSpec((tm, d), lambda i: (i, 0)),
            _const_spec(wdw3.shape),
            _const_spec(bdw3.shape),
            _const_spec((1, c)),
            _const_spec((1, c)),
            _const_spec((c, d)),
            _const_spec((1, d)),
            _const_spec((1, d)),
        ],
        out_specs=pl.BlockSpec((tm, d), lambda i: (i, 0)),
        out_shape=jax.ShapeDtypeStruct((t, d), F32),
        scratch_shapes=[pltpu.VMEM((nc, HALO_ROWS + tm, LANES), F32),
                        pltpu.VMEM((nc, tm, LANES), F32),
                        pltpu.VMEM((tm, c), BF16)],
        compiler_params=_params(1),
        name="conv_tail",
    )(u, u, sz, x2, wdw3, bdw3, ln_g, ln_b, w, b_out, pg)


def _rope_tables(seq, head_dim):
    half = head_dim // 2
    inv_freq = 1.0 / (ROPE_THETA ** (jnp.arange(half, dtype=F32) * 2.0 / head_dim))
    ang = jnp.arange(seq, dtype=F32)[:, None] * inv_freq[None, :]
    cos, sin = jnp.cos(ang), jnp.sin(ang)
    k_tab = jnp.stack([jnp.concatenate([cos, cos], -1), jnp.concatenate([-sin, sin], -1)])
    return jnp.stack([k_tab * head_dim ** -0.5, k_tab])


def kernel(x, pre_norm_g, post_norm_g, attn_w_in, attn_w_out, conv_w_in, conv_b_in, conv_w_dw,
           conv_b_dw, conv_ln_g, conv_ln_b, conv_w_out, conv_b_out):
    batch, seq, d = x.shape
    depth = pre_norm_g.shape[0]
    x2 = x.reshape(batch * seq, d)
    row = lambda v: v.reshape(1, -1)
    for i in range(depth):
        li = i // N_MIXERS
        pre_g, post_g = row(pre_norm_g[i]), row(post_norm_g[i])
        if i % N_MIXERS == 0:
            width = attn_w_out.shape[1]
            tab = _rope_tables(seq, width // N_HEADS)
            proj = _attn_in(x2, pre_g, attn_w_in[li].astype(BF16), tab, seq)
            o = _moba(proj, batch, seq, width)
            x2 = _attn_out(o, proj, x2, attn_w_out[li].astype(BF16), post_g)
        else:
            c = conv_w_out.shape[1]
            nc = c // LANES
            wdw3 = jnp.pad(conv_w_dw[li], ((0, HALO_ROWS - CONV_K), (0, 0)))
            wdw3 = wdw3.reshape(HALO_ROWS, nc, LANES).transpose(1, 0, 2)
            bdw3 = conv_b_dw[li].reshape(nc, 1, LANES)
            u, sz = _conv_in(x2, pre_g, conv_w_in[li].astype(BF16), row(conv_b_in[li]))
            x2 = _conv_tail(u, sz, x2, wdw3, bdw3, row(conv_ln_g[li]), row(conv_ln_b[li]),
                            conv_w_out[li].astype(BF16), row(conv_b_out[li]), post_g, seq)
    return x2.reshape(batch, seq, d)
```

```python
import functools

import jax
import jax.numpy as jnp
from jax import lax
from jax.experimental import pallas as pl
from jax.experimental.pallas import tpu as pltpu

F32 = jnp.float32
BF16 = jnp.bfloat16

N_MIXERS = 2
N_HEADS = 16
MOBA_BLOCK = 256
MOBA_TOPK = 3
ROPE_THETA = 10000.0
CONV_K = 31
RMS_EPS = 1e-6
LN_EPS = 1e-5
NEG = -1e30
LOG2_E = 1.4426950408889634

LANES = 128
HALO_ROWS = 32
VMEM_LIMIT_BYTES = 56 * 1024 * 1024

_NT = (((1,), (1,)), ((), ()))


def _params(n_axes):
    return pltpu.CompilerParams(
        dimension_semantics=("arbitrary",) * n_axes,
        vmem_limit_bytes=VMEM_LIMIT_BYTES)


def _const_spec(shape):
    return pl.BlockSpec(shape, lambda *_: (0,) * len(shape), pipeline_mode=pl.Buffered(1))


def _rms_norm_rows(x_ref, g_ref, hn_ref, row_chunk):
    def body(r, carry):
        r0 = pl.multiple_of(r * row_chunk, row_chunk)
        xf = x_ref[pl.ds(r0, row_chunk), :]
        ms = jnp.mean(xf * xf, axis=-1, keepdims=True)
        hn_ref[pl.ds(r0, row_chunk), :] = (xf * lax.rsqrt(ms + RMS_EPS) * g_ref[...]).astype(BF16)
        return carry
    lax.fori_loop(0, x_ref.shape[0] // row_chunk, body, 0)


def _silu(t):
    return t * jax.nn.sigmoid(t)


def _attn_in_kernel(x_ref, g_ref, w_ref, tab_ref, o_ref, hn_ref, *, rope_tiles):
    j = pl.program_id(1)

    @pl.when(j == 0)
    def _():
        _rms_norm_rows(x_ref, g_ref, hn_ref, 64)

    acc = jnp.dot(hn_ref[...], w_ref[...], preferred_element_type=F32)

    @pl.when(j < rope_tiles)
    def _():
        cos = tab_ref[0, 0]
        sin = tab_ref[0, 1]
        for h in range(acc.shape[1] // LANES):
            c = acc[:, h * LANES:(h + 1) * LANES]
            rot = pltpu.roll(c, LANES // 2, 1)
            o_ref[:, h * LANES:(h + 1) * LANES] = (c * cos + rot * sin).astype(BF16)

    @pl.when(j >= rope_tiles)
    def _():
        o_ref[...] = acc.astype(BF16)


def _attn_in(x2, g, w, tab, seq, tm=1024, tn=1024):
    t, d = x2.shape
    n = w.shape[1]
    head_dim = tab.shape[-1]
    qk_tiles = (n // 4) // tn
    tiles_per_seq = seq // tm
    return pl.pallas_call(
        functools.partial(_attn_in_kernel, rope_tiles=2 * qk_tiles),
        grid=(t // tm, n // tn),
        in_specs=[
            pl.BlockSpec((tm, d), lambda i, j: (i, 0)),
            pl.BlockSpec((1, d), lambda i, j: (0, 0)),
            pl.BlockSpec((d, tn), lambda i, j: (0, j)),
            pl.BlockSpec((1, 2, tm, head_dim),
                         lambda i, j: (jnp.minimum(j // qk_tiles, 1), 0, i % tiles_per_seq, 0)),
        ],
        out_specs=pl.BlockSpec((tm, tn), lambda i, j: (i, j)),
        out_shape=jax.ShapeDtypeStruct((t, n), BF16),
        scratch_shapes=[pltpu.VMEM((tm, d), BF16)],
        compiler_params=_params(2),
        name="attn_in",
    )(x2, g, w, tab)


def _moba_kernel(q_ref, k_ref, v_ref, o_ref, vt_ref, *bufs, heads, lookahead):
    s_len = q_ref.shape[0]
    hd = q_ref.shape[1] // heads
    s_refs, p_refs = bufs[:lookahead + 1], bufs[lookahead + 1:]
    nb = s_len // MOBA_BLOCK
    blk = MOBA_BLOCK
    ext = vt_ref.shape[1]

    for h in range(heads):
        for n in range(nb):
            vb = v_ref[n * blk:(n + 1) * blk, h * hd:(h + 1) * hd].astype(F32)
            vt_ref[h, 0:hd, n * blk:(n + 1) * blk] = vb.T.astype(BF16)
        vt_ref[h, hd:ext, :] = jnp.ones((ext - hd, s_len), BF16)

    key_pos = lax.broadcasted_iota(jnp.int32, (blk, blk), 0)
    qry_pos = lax.broadcasted_iota(jnp.int32, (blk, blk), 1)
    causal = key_pos <= qry_pos
    blk_id = lax.broadcasted_iota(jnp.int32, (nb, blk), 0)

    def scores(u):
        h, j = units[u]
        cols = slice(h * hd, (h + 1) * hd)
        q_j = q_ref[j * blk:(j + 1) * blk, cols]
        s_all = lax.dot_general(k_ref[0:(j + 1) * blk, cols], q_j, _NT,
                                preferred_element_type=F32)
        maxes, sums = [], []
        for n in range(j + 1):
            rows = slice(n * blk, (n + 1) * blk)
            s_n = s_all[rows, :]
            if n == j:
                s_n = jnp.where(causal, s_n, NEG)
            else:
                sums.append(jnp.sum(s_n, axis=0, keepdims=True))
            maxes.append(jnp.max(s_n, axis=0, keepdims=True))
            s_refs[u % len(s_refs)][rows, :] = s_n
        m_run = maxes[j]
        if j == 0:
            return None, m_run
        pad = jnp.zeros((nb - j, blk), F32)
        gate = jnp.concatenate(sums + [pad], axis=0)
        beaten = jnp.zeros((nb, blk), jnp.int32)
        for m in range(j):
            g_m = gate[m:m + 1, :]
            beats = (g_m > gate) | ((g_m == gate) & (m < blk_id))
            beaten = beaten + beats.astype(jnp.int32)
        bias = jnp.where((beaten < MOBA_TOPK) & (blk_id < j), 0.0, NEG)
        for n in range(j):
            m_run = jnp.maximum(m_run, maxes[n] + bias[n:n + 1, :])
        return bias, m_run

    units = [(h, j) for h in range(heads) for j in range(nb)]
    pending = {u: scores(u) for u in range(min(lookahead, len(units)))}
    for u, (h, j) in enumerate(units):
        if u + lookahead < len(units):
            pending[u + lookahead] = scores(u + lookahead)
        bias, m_run = pending.pop(u)
        s_ref, p_ref = s_refs[u % len(s_refs)], p_refs[u % len(p_refs)]
        for n in range(j + 1):
            rows = slice(n * blk, (n + 1) * blk)
            shift = m_run if n == j else m_run - bias[n:n + 1, :]
            p_ref[rows, :] = jnp.exp2(s_ref[rows, :] - shift).astype(BF16)
        acc = jnp.dot(vt_ref[h, :, 0:(j + 1) * blk], p_ref[0:(j + 1) * blk, :],
                      preferred_element_type=F32)
        out_t = acc[0:hd, :] * (1.0 / acc[hd:hd + 1, :])
        o_ref[j * blk:(j + 1) * blk, h * hd:(h + 1) * hd] = out_t.T.astype(BF16)


def _moba(proj, batch, seq, width, heads=2, lookahead=2):
    hd = width // N_HEADS
    t = proj.shape[0]
    ext = hd + 16
    groups = N_HEADS // heads
    return pl.pallas_call(
        functools.partial(_moba_kernel, heads=heads, lookahead=lookahead),
        grid=(batch, groups),
        in_specs=[
            pl.BlockSpec((seq, heads * hd), lambda b, g: (b, g)),
            pl.BlockSpec((seq, heads * hd), lambda b, g: (b, groups + g)),
            pl.BlockSpec((seq, heads * hd), lambda b, g: (b, 2 * groups + g)),
        ],
        out_specs=pl.BlockSpec((seq, heads * hd), lambda b, g: (b, g)),
        out_shape=jax.ShapeDtypeStruct((t, width), BF16),
        scratch_shapes=([pltpu.VMEM((heads, ext, seq), BF16)]
                        + [pltpu.VMEM((seq, MOBA_BLOCK), F32)] * (lookahead + 1)
                        + [pltpu.VMEM((seq, MOBA_BLOCK), BF16)] * 2),
        compiler_params=_params(2),
        name="moba",
    )(proj, proj, proj)


def _post_norm_residual(y, x_ref, pg_ref, o_ref):
    ms = jnp.mean(y * y, axis=-1, keepdims=True)
    o_ref[...] = x_ref[...] + y * lax.rsqrt(ms + RMS_EPS) * pg_ref[...]


def _attn_out_kernel(o_in_ref, z_ref, x_ref, w_ref, pg_ref, o_ref):
    gated = (o_in_ref[...].astype(F32) * _silu(z_ref[...].astype(F32))).astype(BF16)
    y = jnp.dot(gated, w_ref[...], preferred_element_type=F32)
    _post_norm_residual(y, x_ref, pg_ref, o_ref)


def _attn_out(o, proj, x2, w, pg, tm=512):
    t, d = x2.shape
    width = o.shape[1]
    z_block = proj.shape[1] // width - 1
    return pl.pallas_call(
        _attn_out_kernel,
        grid=(t // tm,),
        in_specs=[
            pl.BlockSpec((tm, width), lambda i: (i, 0)),
            pl.BlockSpec((tm, width), lambda i: (i, z_block)),
            pl.BlockSpec((tm, d), lambda i: (i, 0)),
            _const_spec((width, d)),
            _const_spec((1, d)),
        ],
        out_specs=pl.BlockSpec((tm, d), lambda i: (i, 0)),
        out_shape=jax.ShapeDtypeStruct((t, d), F32),
        compiler_params=_params(1),
        name="attn_out",
    )(o, proj, x2, w, pg)


def _conv_in_kernel(x_ref, g_ref, wa_ref, wg_ref, wz_ref, ba_ref, bg_ref, u_ref, sz_ref, hn_ref):
    @pl.when(pl.program_id(1) == 0)
    def _():
        _rms_norm_rows(x_ref, g_ref, hn_ref, 64)

    hn = hn_ref[...]
    a = jnp.dot(hn, wa_ref[...], preferred_element_type=F32) + ba_ref[...]
    g = jnp.dot(hn, wg_ref[...], preferred_element_type=F32) + bg_ref[...]
    u_ref[...] = (a * jax.nn.sigmoid(g)).astype(BF16)
    z = jnp.dot(hn, wz_ref[...], preferred_element_type=F32)
    sz_ref[...] = _silu(z).astype(BF16)


def _conv_in(x2, g, w, b_in, tm=1024, tn=512):
    t, d = x2.shape
    c = w.shape[1] // 3
    nj = c // tn
    return pl.pallas_call(
        _conv_in_kernel,
        grid=(t // tm, nj),
        in_specs=[
            pl.BlockSpec((tm, d), lambda i, j: (i, 0)),
            pl.BlockSpec((1, d), lambda i, j: (0, 0)),
            pl.BlockSpec((d, tn), lambda i, j: (0, j)),
            pl.BlockSpec((d, tn), lambda i, j: (0, nj + j)),
            pl.BlockSpec((d, tn), lambda i, j: (0, 2 * nj + j)),
            pl.BlockSpec((1, tn), lambda i, j: (0, j)),
            pl.BlockSpec((1, tn), lambda i, j: (0, nj + j)),
        ],
        out_specs=[pl.BlockSpec((tm, tn), lambda i, j: (i, j)),
                   pl.BlockSpec((tm, tn), lambda i, j: (i, j))],
        out_shape=[jax.ShapeDtypeStruct((t, c), BF16), jax.ShapeDtypeStruct((t, c), BF16)],
        scratch_shapes=[pltpu.VMEM((tm, d), BF16)],
        compiler_params=_params(2),
        name="conv_in",
    )(x2, g, w, w, w, b_in, b_in)


def _conv_tail_kernel(u_ref, halo_ref, sz_ref, x_ref, wdw_ref, bdw_ref, lng_ref, lnb_ref,
                      w_ref, bo_ref, pg_ref, o_ref, hist_ref, conv_ref, gated_ref,
                      *, tiles_per_seq, row_chunk):
    tm, c = u_ref.shape
    nc = c // LANES

    first = (pl.program_id(0) % tiles_per_seq) == 0
    halo = jnp.where(first, 0.0, halo_ref[...].astype(F32))
    for ci in range(nc):
        hist_ref[ci, 0:HALO_ROWS, :] = halo[:, ci * LANES:(ci + 1) * LANES]
        hist_ref[ci, HALO_ROWS:HALO_ROWS + tm, :] = u_ref[:, ci * LANES:(ci + 1) * LANES].astype(F32)

    first_tap = HALO_ROWS - (CONV_K - 1)

    def lane_chunk(ci, carry):
        w = wdw_ref[ci]
        taps = [jnp.broadcast_to(w[k:k + 1, :], (row_chunk, LANES)) for k in range(CONV_K)]
        b0 = jnp.broadcast_to(bdw_ref[ci], (row_chunk, LANES))
        for r in range(tm // row_chunk):
            acc = b0
            for k in range(CONV_K):
                acc = acc + hist_ref[ci, pl.ds(r * row_chunk + first_tap + k, row_chunk), :] * taps[k]
            conv_ref[ci, pl.ds(r * row_chunk, row_chunk), :] = acc
        return carry
    lax.fori_loop(0, nc, lane_chunk, 0)

    inv_c = 1.0 / c
    s1 = conv_ref[0]
    for ci in range(1, nc):
        s1 = s1 + conv_ref[ci]
    mu = jnp.sum(s1, axis=1, keepdims=True) * inv_c
    s2 = jnp.zeros((tm, LANES), F32)
    for ci in range(nc):
        dlt = conv_ref[ci] - mu
        s2 = s2 + dlt * dlt
    rstd = lax.rsqrt(jnp.sum(s2, axis=1, keepdims=True) * inv_c + LN_EPS)
    for ci in range(nc):
        sl = slice(ci * LANES, (ci + 1) * LANES)
        y = (conv_ref[ci] - mu) * rstd * lng_ref[:, sl] + lnb_ref[:, sl]
        gated_ref[:, sl] = (_silu(y) * sz_ref[:, sl].astype(F32)).astype(BF16)

    y = jnp.dot(gated_ref[...], w_ref[...], preferred_element_type=F32) + bo_ref[...]
    _post_norm_residual(y, x_ref, pg_ref, o_ref)


def _conv_tail(u, sz, x2, wdw3, bdw3, ln_g, ln_b, w, b_out, pg, seq, tm=512, row_chunk=64):
    t, d = x2.shape
    c = u.shape[1]
    nc = c // LANES
    halo_per_tile = tm // HALO_ROWS
    return pl.pallas_call(
        functools.partial(_conv_tail_kernel, tiles_per_seq=seq // tm, row_chunk=row_chunk),
        grid=(t // tm,),
        in_specs=[
            pl.BlockSpec((tm, c), lambda i: (i, 0)),
            pl.BlockSpec((HALO_ROWS, c), lambda i: (jnp.maximum(i * halo_per_tile - 1, 0), 0)),
            pl.BlockSpec((tm, c), lambda i: (i, 0)),
            pl.BlockSpec((tm, d), lambda i: (i, 0)),
            _const_spec(wdw3.shape),
            _const_spec(bdw3.shape),
            _const_spec((1, c)),
            _const_spec((1, c)),
            _const_spec((c, d)),
            _const_spec((1, d)),
            _const_spec((1, d)),
        ],
        out_specs=pl.BlockSpec((tm, d), lambda i: (i, 0)),
        out_shape=jax.ShapeDtypeStruct((t, d), F32),
        scratch_shapes=[pltpu.VMEM((nc, HALO_ROWS + tm, LANES), F32),
                        pltpu.VMEM((nc, tm, LANES), F32),
                        pltpu.VMEM((tm, c), BF16)],
        compiler_params=_params(1),
        name="conv_tail",
    )(u, u, sz, x2, wdw3, bdw3, ln_g, ln_b, w, b_out, pg)


def _rope_tables(seq, head_dim):
    half = head_dim // 2
    inv_freq = 1.0 / (ROPE_THETA ** (jnp.arange(half, dtype=F32) * 2.0 / head_dim))
    ang = jnp.arange(seq, dtype=F32)[:, None] * inv_freq[None, :]
    cos, sin = jnp.cos(ang), jnp.sin(ang)
    k_tab = jnp.stack([jnp.concatenate([cos, cos], -1), jnp.concatenate([-sin, sin], -1)])
    return jnp.stack([k_tab * (head_dim ** -0.5 * LOG2_E), k_tab])


def kernel(x, pre_norm_g, post_norm_g, attn_w_in, attn_w_out, conv_w_in, conv_b_in, conv_w_dw,
           conv_b_dw, conv_ln_g, conv_ln_b, conv_w_out, conv_b_out):
    batch, seq, d = x.shape
    depth = pre_norm_g.shape[0]
    x2 = x.reshape(batch * seq, d)
    row = lambda v: v.reshape(1, -1)
    for i in range(depth):
        li = i // N_MIXERS
        pre_g, post_g = row(pre_norm_g[i]), row(post_norm_g[i])
        if i % N_MIXERS == 0:
            width = attn_w_out.shape[1]
            tab = _rope_tables(seq, width // N_HEADS)
            proj = _attn_in(x2, pre_g, attn_w_in[li].astype(BF16), tab, seq)
            o = _moba(proj, batch, seq, width)
            x2 = _attn_out(o, proj, x2, attn_w_out[li].astype(BF16), post_g)
        else:
            c = conv_w_out.shape[1]
            nc = c // LANES
            wdw3 = jnp.pad(conv_w_dw[li], ((0, HALO_ROWS - CONV_K), (0, 0)))
            wdw3 = wdw3.reshape(HALO_ROWS, nc, LANES).transpose(1, 0, 2)
            bdw3 = conv_b_dw[li].reshape(nc, 1, LANES)
            u, sz = _conv_in(x2, pre_g, conv_w_in[li].astype(BF16), row(conv_b_in[li]))
            x2 = _conv_tail(u, sz, x2, wdw3, bdw3, row(conv_ln_g[li]), row(conv_ln_b[li]),
                            conv_w_out[li].astype(BF16), row(conv_b_out[li]), post_g, seq)
    return x2.reshape(batch, seq, d)
```

```python
import functools

import jax
import jax.numpy as jnp
from jax import lax
from jax.experimental import pallas as pl
from jax.experimental.pallas import tpu as pltpu

F32 = jnp.float32
BF16 = jnp.bfloat16

N_MIXERS = 2
N_HEADS = 16
MOBA_BLOCK = 256
MOBA_TOPK = 3
ROPE_THETA = 10000.0
CONV_K = 31
RMS_EPS = 1e-6
LN_EPS = 1e-5
NEG = -1e30
LOG2_E = 1.4426950408889634

LANES = 128
SUBLANES = 8
MXU_COLS = 256
PIECE_ROWS = 256
HALO_ROWS = 32
VMEM_LIMIT_BYTES = 56 * 1024 * 1024

_NT = (((1,), (1,)), ((), ()))


def _params(n_axes):
    return pltpu.CompilerParams(
        dimension_semantics=("arbitrary",) * n_axes,
        vmem_limit_bytes=VMEM_LIMIT_BYTES)


def _const_spec(shape):
    return pl.BlockSpec(shape, lambda *_: (0,) * len(shape), pipeline_mode=pl.Buffered(1))


def _rms_norm_rows(x_ref, g_ref, hn_ref, row_chunk):
    def body(r, carry):
        r0 = pl.multiple_of(r * row_chunk, row_chunk)
        xf = x_ref[pl.ds(r0, row_chunk), :]
        ms = jnp.mean(xf * xf, axis=-1, keepdims=True)
        hn_ref[pl.ds(r0, row_chunk), :] = (xf * lax.rsqrt(ms + RMS_EPS) * g_ref[...]).astype(BF16)
        return carry
    lax.fori_loop(0, x_ref.shape[0] // row_chunk, body, 0)


def _silu(t):
    return t * jax.nn.sigmoid(t)


def _attn_in_kernel(x_ref, g_ref, w_ref, tab_ref, o_ref, hn_ref, *, rope_tiles):
    j = pl.program_id(1)

    @pl.when(j == 0)
    def _():
        _rms_norm_rows(x_ref, g_ref, hn_ref, 64)

    @pl.when(j < rope_tiles)
    def _():
        acc = jnp.dot(hn_ref[...], w_ref[...], preferred_element_type=F32)
        cos = tab_ref[0, 0]
        sin = tab_ref[0, 1]
        for h in range(acc.shape[1] // LANES):
            c = acc[:, h * LANES:(h + 1) * LANES]
            rot = pltpu.roll(c, LANES // 2, 1)
            o_ref[:, h * LANES:(h + 1) * LANES] = (c * cos + rot * sin).astype(BF16)

    @pl.when(j >= rope_tiles)
    def _():
        o_ref[...] = jnp.dot(hn_ref[...], w_ref[...], preferred_element_type=F32).astype(BF16)


def _attn_in(x2, g, w, tab, seq, tm=1024, tn=1024):
    t, d = x2.shape
    n = w.shape[1]
    head_dim = tab.shape[-1]
    qk_tiles = (n // 4) // tn
    tiles_per_seq = seq // tm
    return pl.pallas_call(
        functools.partial(_attn_in_kernel, rope_tiles=2 * qk_tiles),
        grid=(t // tm, n // tn),
        in_specs=[
            pl.BlockSpec((tm, d), lambda i, j: (i, 0)),
            pl.BlockSpec((1, d), lambda i, j: (0, 0)),
            pl.BlockSpec((d, tn), lambda i, j: (0, j)),
            pl.BlockSpec((1, 2, tm, head_dim),
                         lambda i, j: (jnp.minimum(j // qk_tiles, 1), 0, i % tiles_per_seq, 0)),
        ],
        out_specs=pl.BlockSpec((tm, tn), lambda i, j: (i, j)),
        out_shape=jax.ShapeDtypeStruct((t, n), BF16),
        scratch_shapes=[pltpu.VMEM((tm, d), BF16)],
        compiler_params=_params(2),
        name="attn_in",
    )(x2, g, w, tab)


def _moba_kernel(q_ref, k_ref, v_ref, o_ref, vt_ref, *bufs, heads, lookahead):
    s_len = q_ref.shape[0]
    hd = q_ref.shape[1] // heads
    s_refs, p_refs = bufs[:lookahead + 1], bufs[lookahead + 1:]
    nb = s_len // MOBA_BLOCK
    blk = MOBA_BLOCK
    ext = vt_ref.shape[1]

    for h in range(heads):
        for n in range(nb):
            vb = v_ref[n * blk:(n + 1) * blk, h * hd:(h + 1) * hd].astype(F32)
            vt_ref[h, 0:hd, n * blk:(n + 1) * blk] = vb.T.astype(BF16)
        vt_ref[h, hd:ext, :] = jnp.ones((ext - hd, s_len), BF16)

    key_pos = lax.broadcasted_iota(jnp.int32, (blk, blk), 0)
    qry_pos = lax.broadcasted_iota(jnp.int32, (blk, blk), 1)
    causal = key_pos <= qry_pos
    blk_id = lax.broadcasted_iota(jnp.int32, (nb, blk), 0)

    def scores(u):
        h, j = units[u]
        cols = slice(h * hd, (h + 1) * hd)
        q_j = q_ref[j * blk:(j + 1) * blk, cols]
        s_all = lax.dot_general(k_ref[0:(j + 1) * blk, cols], q_j, _NT,
                                preferred_element_type=F32)
        maxes, sums = [], []
        for n in range(j + 1):
            rows = slice(n * blk, (n + 1) * blk)
            s_n = s_all[rows, :]
            if n == j:
                s_n = jnp.where(causal, s_n, NEG)
            else:
                sums.append(jnp.sum(s_n, axis=0, keepdims=True))
            maxes.append(jnp.max(s_n, axis=0, keepdims=True))
            s_refs[u % len(s_refs)][rows, :] = s_n
        m_run = maxes[j]
        if j == 0:
            return None, m_run
        pad = jnp.zeros((nb - j, blk), F32)
        gate = jnp.concatenate(sums + [pad], axis=0)
        beaten = jnp.zeros((nb, blk), jnp.int32)
        for m in range(j):
            g_m = gate[m:m + 1, :]
            beats = (g_m > gate) | ((g_m == gate) & (m < blk_id))
            beaten = beaten + beats.astype(jnp.int32)
        bias = jnp.where((beaten < MOBA_TOPK) & (blk_id < j), 0.0, NEG)
        for n in range(j):
            m_run = jnp.maximum(m_run, maxes[n] + bias[n:n + 1, :])
        return bias, m_run

    units = [(h, j) for h in range(heads) for j in range(nb)]
    pending = {u: scores(u) for u in range(min(lookahead, len(units)))}
    for u, (h, j) in enumerate(units):
        if u + lookahead < len(units):
            pending[u + lookahead] = scores(u + lookahead)
        bias, m_run = pending.pop(u)
        s_ref, p_ref = s_refs[u % len(s_refs)], p_refs[u % len(p_refs)]
        for n in range(j + 1):
            rows = slice(n * blk, (n + 1) * blk)
            shift = m_run if n == j else m_run - bias[n:n + 1, :]
            p_ref[rows, :] = jnp.exp2(s_ref[rows, :] - shift).astype(BF16)
        acc = jnp.dot(vt_ref[h, :, 0:(j + 1) * blk], p_ref[0:(j + 1) * blk, :],
                      preferred_element_type=F32)
        out_t = acc[0:hd, :] * (1.0 / acc[hd:hd + 1, :])
        o_ref[j * blk:(j + 1) * blk, h * hd:(h + 1) * hd] = out_t.T.astype(BF16)


def _moba(proj, batch, seq, width, heads=2, lookahead=2):
    hd = width // N_HEADS
    t = proj.shape[0]
    ext = hd + 16
    groups = N_HEADS // heads
    return pl.pallas_call(
        functools.partial(_moba_kernel, heads=heads, lookahead=lookahead),
        grid=(batch, groups),
        in_specs=[
            pl.BlockSpec((seq, heads * hd), lambda b, g: (b, g)),
            pl.BlockSpec((seq, heads * hd), lambda b, g: (b, groups + g)),
            pl.BlockSpec((seq, heads * hd), lambda b, g: (b, 2 * groups + g)),
        ],
        out_specs=pl.BlockSpec((seq, heads * hd), lambda b, g: (b, g)),
        out_shape=jax.ShapeDtypeStruct((t, width), BF16),
        scratch_shapes=([pltpu.VMEM((heads, ext, seq), BF16)]
                        + [pltpu.VMEM((seq, MOBA_BLOCK), F32)] * (lookahead + 1)
                        + [pltpu.VMEM((seq, MOBA_BLOCK), BF16)] * 2),
        compiler_params=_params(2),
        name="moba",
    )(proj, proj, proj)


def _post_norm_residual(y, x_ref, pg_ref, o_ref):
    ms = jnp.mean(y * y, axis=-1, keepdims=True)
    o_ref[...] = x_ref[...] + y * lax.rsqrt(ms + RMS_EPS) * pg_ref[...]


def _attn_out_kernel(o_in_ref, z_ref, x_ref, w_ref, pg_ref, o_ref):
    gated = (o_in_ref[...].astype(F32) * _silu(z_ref[...].astype(F32))).astype(BF16)
    y = jnp.dot(gated, w_ref[...], preferred_element_type=F32)
    _post_norm_residual(y, x_ref, pg_ref, o_ref)


def _attn_out(o, proj, x2, w, pg, tm=512):
    t, d = x2.shape
    width = o.shape[1]
    z_block = proj.shape[1] // width - 1
    return pl.pallas_call(
        _attn_out_kernel,
        grid=(t // tm,),
        in_specs=[
            pl.BlockSpec((tm, width), lambda i: (i, 0)),
            pl.BlockSpec((tm, width), lambda i: (i, z_block)),
            pl.BlockSpec((tm, d), lambda i: (i, 0)),
            _const_spec((width, d)),
            _const_spec((1, d)),
        ],
        out_specs=pl.BlockSpec((tm, d), lambda i: (i, 0)),
        out_shape=jax.ShapeDtypeStruct((t, d), F32),
        compiler_params=_params(1),
        name="attn_out",
    )(o, proj, x2, w, pg)


def _conv_in_kernel(x_ref, g_ref, wa_ref, wg_ref, wz_ref, ba_ref, bg_ref, wdw_ref, bdw_ref,
                    c_ref, sz_ref, hn_ref, ucur_ref, uprev_ref, halo_ref,
                    *, nj, last, tiles_per_seq, row_chunk):
    s = pl.program_id(0)
    cur = jnp.minimum(s, last)
    i, j = cur // nj, cur % nj
    lane_chunks, tm, _ = ucur_ref.shape
    tn = lane_chunks * LANES

    @pl.when(s == 0)
    def _():
        uprev_ref[...] = jnp.zeros_like(uprev_ref)
        halo_ref[...] = jnp.zeros_like(halo_ref)

    @pl.when((j == 0) & (s <= last))
    def _():
        _rms_norm_rows(x_ref, g_ref, hn_ref, 64)

    first_tap = HALO_ROWS - (CONV_K - 1)

    def conv_items():
        for ci in range(tn // LANES):
            lanes = slice(ci * LANES, (ci + 1) * LANES)
            b0 = jnp.broadcast_to(bdw_ref[ci], (row_chunk, LANES))
            for r in range(tm // row_chunk):
                acc = b0
                for k in range(CONV_K):
                    tap = wdw_ref[ci, pl.ds(k * SUBLANES, SUBLANES), :]
                    tap = jnp.concatenate([tap] * (row_chunk // SUBLANES), axis=0)
                    start = r * row_chunk + first_tap + k
                    acc = acc + uprev_ref[ci, pl.ds(start, row_chunk), :] * tap
                c_ref[r * row_chunk:(r + 1) * row_chunk, lanes] = acc.astype(BF16)
                yield

    pieces = [(slice(rp * PIECE_ROWS, (rp + 1) * PIECE_ROWS), slice(cp * MXU_COLS, (cp + 1) * MXU_COLS))
              for cp in range(tn // MXU_COLS) for rp in range(tm // PIECE_ROWS)]
    conv = conv_items()
    n_items = (tn // LANES) * (tm // row_chunk)
    n_slots = 3 * len(pieces)
    done = 0

    def conv_share(slot):
        nonlocal done
        target = (slot + 1) * n_items // n_slots
        while done < target:
            next(conv)
            done += 1

    slot = 0
    for rows, cols in pieces:
        hn = hn_ref[rows, :]
        a = jnp.dot(hn, wa_ref[:, cols], preferred_element_type=F32) + ba_ref[:, cols]
        conv_share(slot)
        g = jnp.dot(hn, wg_ref[:, cols], preferred_element_type=F32) + bg_ref[:, cols]
        u = a * jax.nn.sigmoid(g)
        for h in range(MXU_COLS // LANES):
            ucur_ref[cols.start // LANES + h, rows, :] = u[:, h * LANES:(h + 1) * LANES]
        conv_share(slot + 1)
        z = jnp.dot(hn, wz_ref[:, cols], preferred_element_type=F32)
        sz_ref[rows, cols] = _silu(z).astype(BF16)
        conv_share(slot + 2)
        slot += 3

    first = (i % tiles_per_seq) == 0
    uprev_ref[:, 0:HALO_ROWS, :] = jnp.where(first, 0.0, halo_ref[j])
    uprev_ref[:, HALO_ROWS:, :] = ucur_ref[...]
    halo_ref[j] = ucur_ref[:, tm - HALO_ROWS:, :]


def _conv_in(x2, g, w, b_in, wdw3, bdw3, seq, tm=1024, tn=512, row_chunk=64):
    t, d = x2.shape
    c = w.shape[1] // 3
    nj = c // tn
    last = (t // tm) * nj - 1
    cur = lambda s: jnp.minimum(s, last)
    prev = lambda s: jnp.maximum(s - 1, 0)
    lane_chunks = tn // LANES
    return pl.pallas_call(
        functools.partial(_conv_in_kernel, nj=nj, last=last, tiles_per_seq=seq // tm,
                          row_chunk=row_chunk),
        grid=(last + 2,),
        in_specs=[
            pl.BlockSpec((tm, d), lambda s: (cur(s) // nj, 0)),
            pl.BlockSpec((1, d), lambda s: (0, 0)),
            pl.BlockSpec((d, tn), lambda s: (0, cur(s) % nj)),
            pl.BlockSpec((d, tn), lambda s: (0, nj + cur(s) % nj)),
            pl.BlockSpec((d, tn), lambda s: (0, 2 * nj + cur(s) % nj)),
            pl.BlockSpec((1, tn), lambda s: (0, cur(s) % nj)),
            pl.BlockSpec((1, tn), lambda s: (0, nj + cur(s) % nj)),
            pl.BlockSpec((lane_chunks, HALO_ROWS * SUBLANES, LANES), lambda s: (prev(s) % nj, 0, 0)),
            pl.BlockSpec((lane_chunks, 1, LANES), lambda s: (prev(s) % nj, 0, 0)),
        ],
        out_specs=[pl.BlockSpec((tm, tn), lambda s: (prev(s) // nj, prev(s) % nj)),
                   pl.BlockSpec((tm, tn), lambda s: (cur(s) // nj, cur(s) % nj))],
        out_shape=[jax.ShapeDtypeStruct((t, c), BF16), jax.ShapeDtypeStruct((t, c), BF16)],
        scratch_shapes=[pltpu.VMEM((tm, d), BF16),
                        pltpu.VMEM((lane_chunks, tm, LANES), F32),
                        pltpu.VMEM((lane_chunks, HALO_ROWS + tm, LANES), F32),
                        pltpu.VMEM((nj, lane_chunks, HALO_ROWS, LANES), F32)],
        compiler_params=_params(1),
        name="conv_in",
    )(x2, g, w, w, w, b_in, b_in, wdw3, bdw3)


def _conv_tail_kernel(c_ref, sz_ref, x_ref, lng_ref, lnb_ref, w_ref, bo_ref, pg_ref, o_ref, gated_ref):
    tm, c = c_ref.shape
    chunks = [slice(ci * LANES, (ci + 1) * LANES) for ci in range(c // LANES)]

    inv_c = 1.0 / c
    s1 = jnp.zeros((tm, LANES), F32)
    for sl in chunks:
        s1 = s1 + c_ref[:, sl].astype(F32)
    mu = jnp.sum(s1, axis=1, keepdims=True) * inv_c
    s2 = jnp.zeros((tm, LANES), F32)
    for sl in chunks:
        dlt = c_ref[:, sl].astype(F32) - mu
        s2 = s2 + dlt * dlt
    rstd = lax.rsqrt(jnp.sum(s2, axis=1, keepdims=True) * inv_c + LN_EPS)
    for sl in chunks:
        y = (c_ref[:, sl].astype(F32) - mu) * rstd * lng_ref[:, sl] + lnb_ref[:, sl]
        gated_ref[:, sl] = (_silu(y) * sz_ref[:, sl].astype(F32)).astype(BF16)

    y = jnp.dot(gated_ref[...], w_ref[...], preferred_element_type=F32) + bo_ref[...]
    _post_norm_residual(y, x_ref, pg_ref, o_ref)


def _conv_tail(conv, sz, x2, ln_g, ln_b, w, b_out, pg, tm=512):
    t, d = x2.shape
    c = conv.shape[1]
    return pl.pallas_call(
        _conv_tail_kernel,
        grid=(t // tm,),
        in_specs=[
            pl.BlockSpec((tm, c), lambda i: (i, 0)),
            pl.BlockSpec((tm, c), lambda i: (i, 0)),
            pl.BlockSpec((tm, d), lambda i: (i, 0)),
            _const_spec((1, c)),
            _const_spec((1, c)),
            _const_spec((c, d)),
            _const_spec((1, d)),
            _const_spec((1, d)),
        ],
        out_specs=pl.BlockSpec((tm, d), lambda i: (i, 0)),
        out_shape=jax.ShapeDtypeStruct((t, d), F32),
        scratch_shapes=[pltpu.VMEM((tm, c), BF16)],
        compiler_params=_params(1),
        name="conv_tail",
    )(conv, sz, x2, ln_g, ln_b, w, b_out, pg)


def _rope_tables(seq, head_dim):
    half = head_dim // 2
    inv_freq = 1.0 / (ROPE_THETA ** (jnp.arange(half, dtype=F32) * 2.0 / head_dim))
    ang = jnp.arange(seq, dtype=F32)[:, None] * inv_freq[None, :]
    cos, sin = jnp.cos(ang), jnp.sin(ang)
    k_tab = jnp.stack([jnp.concatenate([cos, cos], -1), jnp.concatenate([-sin, sin], -1)])
    return jnp.stack([k_tab * (head_dim ** -0.5 * LOG2_E), k_tab])


def kernel(x, pre_norm_g, post_norm_g, attn_w_in, attn_w_out, conv_w_in, conv_b_in, conv_w_dw,
           conv_b_dw, conv_ln_g, conv_ln_b, conv_w_out, conv_b_out):
    batch, seq, d = x.shape
    depth = pre_norm_g.shape[0]
    x2 = x.reshape(batch * seq, d)
    row = lambda v: v.reshape(1, -1)
    for i in range(depth):
        li = i // N_MIXERS
        pre_g, post_g = row(pre_norm_g[i]), row(post_norm_g[i])
        if i % N_MIXERS == 0:
            width = attn_w_out.shape[1]
            tab = _rope_tables(seq, width // N_HEADS)
            proj = _attn_in(x2, pre_g, attn_w_in[li].astype(BF16), tab, seq)
            o = _moba(proj, batch, seq, width)
            x2 = _attn_out(o, proj, x2, attn_w_out[li].astype(BF16), post_g)
        else:
            c = conv_w_out.shape[1]
            nc = c // LANES
            wdw3 = jnp.pad(conv_w_dw[li], ((0, HALO_ROWS - CONV_K), (0, 0)))
            wdw3 = wdw3.reshape(HALO_ROWS, nc, LANES).transpose(1, 0, 2)
            wdw3 = jnp.repeat(wdw3, SUBLANES, axis=1)
            bdw3 = conv_b_dw[li].reshape(nc, 1, LANES)
            conv, sz = _conv_in(x2, pre_g, conv_w_in[li].astype(BF16), row(conv_b_in[li]),
                                wdw3, bdw3, seq)
            x2 = _conv_tail(conv, sz, x2, row(conv_ln_g[li]), row(conv_ln_b[li]),
                            conv_w_out[li].astype(BF16), row(conv_b_out[li]), post_g)
    return x2.reshape(batch, seq, d)
```

```python
import functools

import jax
import jax.numpy as jnp
from jax import lax
from jax.experimental import pallas as pl
from jax.experimental.pallas import tpu as pltpu

F32 = jnp.float32
BF16 = jnp.bfloat16

N_MIXERS = 2
N_HEADS = 16
MOBA_BLOCK = 256
MOBA_TOPK = 3
ROPE_THETA = 10000.0
CONV_K = 31
RMS_EPS = 1e-6
LN_EPS = 1e-5
NEG = -1e30
LOG2_E = 1.4426950408889634

LANES = 128
HALO_ROWS = 32
VMEM_LIMIT_BYTES = 56 * 1024 * 1024

_NT = (((1,), (1,)), ((), ()))


def _params(n_axes):
    return pltpu.CompilerParams(
        dimension_semantics=("arbitrary",) * n_axes,
        vmem_limit_bytes=VMEM_LIMIT_BYTES)


def _const_spec(shape):
    return pl.BlockSpec(shape, lambda *_: (0,) * len(shape), pipeline_mode=pl.Buffered(1))


def _rms_norm_rows(x_ref, g_ref, hn_ref, row_chunk):
    def body(r, carry):
        r0 = pl.multiple_of(r * row_chunk, row_chunk)
        xf = x_ref[pl.ds(r0, row_chunk), :]
        ms = jnp.mean(xf * xf, axis=-1, keepdims=True)
        hn_ref[pl.ds(r0, row_chunk), :] = (xf * lax.rsqrt(ms + RMS_EPS) * g_ref[...]).astype(BF16)
        return carry
    lax.fori_loop(0, x_ref.shape[0] // row_chunk, body, 0)


def _silu(t):
    return t * jax.nn.sigmoid(t)


def _attn_in_kernel(x_ref, g_ref, w_ref, tab_ref, o_ref, hn_ref, *, rope_tiles):
    j = pl.program_id(1)

    @pl.when(j == 0)
    def _():
        _rms_norm_rows(x_ref, g_ref, hn_ref, 64)

    @pl.when(j < rope_tiles)
    def _():
        acc = jnp.dot(hn_ref[...], w_ref[...], preferred_element_type=F32)
        cos = tab_ref[0, 0]
        sin = tab_ref[0, 1]
        for h in range(acc.shape[1] // LANES):
            c = acc[:, h * LANES:(h + 1) * LANES]
            rot = pltpu.roll(c, LANES // 2, 1)
            o_ref[:, h * LANES:(h + 1) * LANES] = (c * cos + rot * sin).astype(BF16)

    @pl.when(j >= rope_tiles)
    def _():
        o_ref[...] = jnp.dot(hn_ref[...], w_ref[...], preferred_element_type=F32).astype(BF16)


def _attn_in(x2, g, w, tab, seq, tm=1024, tn=1024):
    t, d = x2.shape
    n = w.shape[1]
    head_dim = tab.shape[-1]
    qk_tiles = (n // 4) // tn
    tiles_per_seq = seq // tm
    return pl.pallas_call(
        functools.partial(_attn_in_kernel, rope_tiles=2 * qk_tiles),
        grid=(t // tm, n // tn),
        in_specs=[
            pl.BlockSpec((tm, d), lambda i, j: (i, 0)),
            pl.BlockSpec((1, d), lambda i, j: (0, 0)),
            pl.BlockSpec((d, tn), lambda i, j: (0, j)),
            pl.BlockSpec((1, 2, tm, head_dim),
                         lambda i, j: (jnp.minimum(j // qk_tiles, 1), 0, i % tiles_per_seq, 0)),
        ],
        out_specs=pl.BlockSpec((tm, tn), lambda i, j: (i, j)),
        out_shape=jax.ShapeDtypeStruct((t, n), BF16),
        scratch_shapes=[pltpu.VMEM((tm, d), BF16)],
        compiler_params=_params(2),
        name="attn_in",
    )(x2, g, w, tab)


def _moba_kernel(q_ref, k_ref, v_ref, o_ref, vt_ref, *bufs, heads, lookahead):
    s_len = q_ref.shape[0]
    hd = q_ref.shape[1] // heads
    s_refs, p_refs = bufs[:lookahead + 1], bufs[lookahead + 1:]
    nb = s_len // MOBA_BLOCK
    blk = MOBA_BLOCK
    ext = vt_ref.shape[1]

    for h in range(heads):
        for n in range(nb):
            vb = v_ref[n * blk:(n + 1) * blk, h * hd:(h + 1) * hd].astype(F32)
            vt_ref[h, 0:hd, n * blk:(n + 1) * blk] = vb.T.astype(BF16)
        vt_ref[h, hd:ext, :] = jnp.ones((ext - hd, s_len), BF16)

    key_pos = lax.broadcasted_iota(jnp.int32, (blk, blk), 0)
    qry_pos = lax.broadcasted_iota(jnp.int32, (blk, blk), 1)
    causal = key_pos <= qry_pos
    blk_id = lax.broadcasted_iota(jnp.int32, (nb, blk), 0)

    def scores(u):
        h, j = units[u]
        cols = slice(h * hd, (h + 1) * hd)
        q_j = q_ref[j * blk:(j + 1) * blk, cols]
        s_all = lax.dot_general(k_ref[0:(j + 1) * blk, cols], q_j, _NT,
                                preferred_element_type=F32)
        maxes, sums = [], []
        for n in range(j + 1):
            rows = slice(n * blk, (n + 1) * blk)
            s_n = s_all[rows, :]
            if n == j:
                s_n = jnp.where(causal, s_n, NEG)
            else:
                sums.append(jnp.sum(s_n, axis=0, keepdims=True))
            maxes.append(jnp.max(s_n, axis=0, keepdims=True))
            s_refs[u % len(s_refs)][rows, :] = s_n
        m_run = maxes[j]
        if j == 0:
            return None, m_run
        pad = jnp.zeros((nb - j, blk), F32)
        gate = jnp.concatenate(sums + [pad], axis=0)
        beaten = jnp.zeros((nb, blk), jnp.int32)
        for m in range(j):
            g_m = gate[m:m + 1, :]
            beats = (g_m > gate) | ((g_m == gate) & (m < blk_id))
            beaten = beaten + beats.astype(jnp.int32)
        bias = jnp.where((beaten < MOBA_TOPK) & (blk_id < j), 0.0, NEG)
        for n in range(j):
            m_run = jnp.maximum(m_run, maxes[n] + bias[n:n + 1, :])
        return bias, m_run

    units = [(h, j) for h in range(heads) for j in range(nb)]
    pending = {u: scores(u) for u in range(min(lookahead, len(units)))}
    for u, (h, j) in enumerate(units):
        if u + lookahead < len(units):
            pending[u + lookahead] = scores(u + lookahead)
        bias, m_run = pending.pop(u)
        s_ref, p_ref = s_refs[u % len(s_refs)], p_refs[u % len(p_refs)]
        for n in range(j + 1):
            rows = slice(n * blk, (n + 1) * blk)
            shift = m_run if n == j else m_run - bias[n:n + 1, :]
            p_ref[rows, :] = jnp.exp2(s_ref[rows, :] - shift).astype(BF16)
        acc = jnp.dot(vt_ref[h, :, 0:(j + 1) * blk], p_ref[0:(j + 1) * blk, :],
                      preferred_element_type=F32)
        out_t = acc[0:hd, :] * (1.0 / acc[hd:hd + 1, :])
        o_ref[j * blk:(j + 1) * blk, h * hd:(h + 1) * hd] = out_t.T.astype(BF16)


def _moba(proj, batch, seq, width, heads=4, lookahead=2):
    hd = width // N_HEADS
    t = proj.shape[0]
    ext = hd + 16
    groups = N_HEADS // heads
    return pl.pallas_call(
        functools.partial(_moba_kernel, heads=heads, lookahead=lookahead),
        grid=(batch, groups),
        in_specs=[
            pl.BlockSpec((seq, heads * hd), lambda b, g: (b, g)),
            pl.BlockSpec((seq, heads * hd), lambda b, g: (b, groups + g)),
            pl.BlockSpec((seq, heads * hd), lambda b, g: (b, 2 * groups + g)),
        ],
        out_specs=pl.BlockSpec((seq, heads * hd), lambda b, g: (b, g)),
        out_shape=jax.ShapeDtypeStruct((t, width), BF16),
        scratch_shapes=([pltpu.VMEM((heads, ext, seq), BF16)]
                        + [pltpu.VMEM((seq, MOBA_BLOCK), F32)] * (lookahead + 1)
                        + [pltpu.VMEM((seq, MOBA_BLOCK), BF16)] * 2),
        compiler_params=_params(2),
        name="moba",
    )(proj, proj, proj)


def _post_norm_residual(y, x_ref, pg_ref, o_ref):
    ms = jnp.mean(y * y, axis=-1, keepdims=True)
    o_ref[...] = x_ref[...] + y * lax.rsqrt(ms + RMS_EPS) * pg_ref[...]


def _attn_out_kernel(o_in_ref, z_ref, x_ref, w_ref, pg_ref, o_ref):
    gated = (o_in_ref[...].astype(F32) * _silu(z_ref[...].astype(F32))).astype(BF16)
    y = jnp.dot(gated, w_ref[...], preferred_element_type=F32)
    _post_norm_residual(y, x_ref, pg_ref, o_ref)


def _attn_out(o, proj, x2, w, pg, tm=512):
    t, d = x2.shape
    width = o.shape[1]
    z_block = proj.shape[1] // width - 1
    return pl.pallas_call(
        _attn_out_kernel,
        grid=(t // tm,),
        in_specs=[
            pl.BlockSpec((tm, width), lambda i: (i, 0)),
            pl.BlockSpec((tm, width), lambda i: (i, z_block)),
            pl.BlockSpec((tm, d), lambda i: (i, 0)),
            _const_spec((width, d)),
            _const_spec((1, d)),
        ],
        out_specs=pl.BlockSpec((tm, d), lambda i: (i, 0)),
        out_shape=jax.ShapeDtypeStruct((t, d), F32),
        compiler_params=_params(1),
        name="attn_out",
    )(o, proj, x2, w, pg)


def _conv_in_kernel(x_ref, g_ref, wa_ref, wg_ref, wz_ref, ba_ref, bg_ref, u_ref, sz_ref, hn_ref):
    @pl.when(pl.program_id(1) == 0)
    def _():
        _rms_norm_rows(x_ref, g_ref, hn_ref, 64)

    hn = hn_ref[...]
    a = jnp.dot(hn, wa_ref[...], preferred_element_type=F32) + ba_ref[...]
    g = jnp.dot(hn, wg_ref[...], preferred_element_type=F32) + bg_ref[...]
    u_ref[...] = (a * jax.nn.sigmoid(g)).astype(BF16)
    z = jnp.dot(hn, wz_ref[...], preferred_element_type=F32)
    sz_ref[...] = _silu(z).astype(BF16)


def _conv_in(x2, g, w, b_in, tm=1024, tn=512):
    t, d = x2.shape
    c = w.shape[1] // 3
    nj = c // tn
    return pl.pallas_call(
        _conv_in_kernel,
        grid=(t // tm, nj),
        in_specs=[
            pl.BlockSpec((tm, d), lambda i, j: (i, 0)),
            pl.BlockSpec((1, d), lambda i, j: (0, 0)),
            pl.BlockSpec((d, tn), lambda i, j: (0, j)),
            pl.BlockSpec((d, tn), lambda i, j: (0, nj + j)),
            pl.BlockSpec((d, tn), lambda i, j: (0, 2 * nj + j)),
            pl.BlockSpec((1, tn), lambda i, j: (0, j)),
            pl.BlockSpec((1, tn), lambda i, j: (0, nj + j)),
        ],
        out_specs=[pl.BlockSpec((tm, tn), lambda i, j: (i, j)),
                   pl.BlockSpec((tm, tn), lambda i, j: (i, j))],
        out_shape=[jax.ShapeDtypeStruct((t, c), BF16), jax.ShapeDtypeStruct((t, c), BF16)],
        scratch_shapes=[pltpu.VMEM((tm, d), BF16)],
        compiler_params=_params(2),
        name="conv_in",
    )(x2, g, w, w, w, b_in, b_in)


def _conv_tail_kernel(u_ref, halo_ref, sz_ref, x_ref, wdw_ref, bdw_ref, lng_ref, lnb_ref,
                      w_ref, bo_ref, pg_ref, o_ref, hist_ref, conv_ref, gated_ref,
                      *, tiles_per_seq, row_chunk):
    tm, c = u_ref.shape
    nc = c // LANES

    first = (pl.program_id(0) % tiles_per_seq) == 0
    halo = jnp.where(first, 0.0, halo_ref[...].astype(F32))
    for ci in range(nc):
        hist_ref[ci, 0:HALO_ROWS, :] = halo[:, ci * LANES:(ci + 1) * LANES]
        hist_ref[ci, HALO_ROWS:HALO_ROWS + tm, :] = u_ref[:, ci * LANES:(ci + 1) * LANES].astype(F32)

    first_tap = HALO_ROWS - (CONV_K - 1)

    def lane_chunk(ci, carry):
        w = wdw_ref[ci]
        taps = [jnp.broadcast_to(w[k:k + 1, :], (row_chunk, LANES)) for k in range(CONV_K)]
        b0 = jnp.broadcast_to(bdw_ref[ci], (row_chunk, LANES))
        for r in range(tm // row_chunk):
            acc = b0
            for k in range(CONV_K):
                acc = acc + hist_ref[ci, pl.ds(r * row_chunk + first_tap + k, row_chunk), :] * taps[k]
            conv_ref[ci, pl.ds(r * row_chunk, row_chunk), :] = acc
        return carry
    lax.fori_loop(0, nc, lane_chunk, 0)

    inv_c = 1.0 / c
    s1 = conv_ref[0]
    for ci in range(1, nc):
        s1 = s1 + conv_ref[ci]
    mu = jnp.sum(s1, axis=1, keepdims=True) * inv_c
    s2 = jnp.zeros((tm, LANES), F32)
    for ci in range(nc):
        dlt = conv_ref[ci] - mu
        s2 = s2 + dlt * dlt
    rstd = lax.rsqrt(jnp.sum(s2, axis=1, keepdims=True) * inv_c + LN_EPS)
    for ci in range(nc):
        sl = slice(ci * LANES, (ci + 1) * LANES)
        y = (conv_ref[ci] - mu) * rstd * lng_ref[:, sl] + lnb_ref[:, sl]
        gated_ref[:, sl] = (_silu(y) * sz_ref[:, sl].astype(F32)).astype(BF16)

    y = jnp.dot(gated_ref[...], w_ref[...], preferred_element_type=F32) + bo_ref[...]
    _post_norm_residual(y, x_ref, pg_ref, o_ref)


def _conv_tail(u, sz, x2, wdw3, bdw3, ln_g, ln_b, w, b_out, pg, seq, tm=512, row_chunk=64):
    t, d = x2.shape
    c = u.shape[1]
    nc = c // LANES
    halo_per_tile = tm // HALO_ROWS
    return pl.pallas_call(
        functools.partial(_conv_tail_kernel, tiles_per_seq=seq // tm, row_chunk=row_chunk),
        grid=(t // tm,),
        in_specs=[
            pl.BlockSpec((tm, c), lambda i: (i, 0)),
            pl.BlockSpec((HALO_ROWS, c), lambda i: (jnp.maximum(i * halo_per_tile - 1, 0), 0)),
            pl.BlockSpec((tm, c), lambda i: (i, 0)),
            pl.BlockSpec((tm, d), lambda i: (i, 0)),
            _const_spec(wdw3.shape),
            _const_spec(bdw3.shape),
            _const_spec((1, c)),
            _const_spec((1, c)),
            _const_spec((c, d)),
            _const_spec((1, d)),
            _const_spec((1, d)),
        ],
        out_specs=pl.BlockSpec((tm, d), lambda i: (i, 0)),
        out_shape=jax.ShapeDtypeStruct((t, d), F32),
        scratch_shapes=[pltpu.VMEM((nc, HALO_ROWS + tm, LANES), F32),
                        pltpu.VMEM((nc, tm, LANES), F32),
                        pltpu.VMEM((tm, c), BF16)],
        compiler_params=_params(1),
        name="conv_tail",
    )(u, u, sz, x2, wdw3, bdw3, ln_g, ln_b, w, b_out, pg)


def _rope_tables(seq, head_dim):
    half = head_dim // 2
    inv_freq = 1.0 / (ROPE_THETA ** (jnp.arange(half, dtype=F32) * 2.0 / head_dim))
    ang = jnp.arange(seq, dtype=F32)[:, None] * inv_freq[None, :]
    cos, sin = jnp.cos(ang), jnp.sin(ang)
    k_tab = jnp.stack([jnp.concatenate([cos, cos], -1), jnp.concatenate([-sin, sin], -1)])
    return jnp.stack([k_tab * (head_dim ** -0.5 * LOG2_E), k_tab])


def kernel(x, pre_norm_g, post_norm_g, attn_w_in, attn_w_out, conv_w_in, conv_b_in, conv_w_dw,
           conv_b_dw, conv_ln_g, conv_ln_b, conv_w_out, conv_b_out):
    batch, seq, d = x.shape
    depth = pre_norm_g.shape[0]
    x2 = x.reshape(batch * seq, d)
    row = lambda v: v.reshape(1, -1)
    for i in range(depth):
        li = i // N_MIXERS
        pre_g, post_g = row(pre_norm_g[i]), row(post_norm_g[i])
        if i % N_MIXERS == 0:
            width = attn_w_out.shape[1]
            tab = _rope_tables(seq, width // N_HEADS)
            proj = _attn_in(x2, pre_g, attn_w_in[li].astype(BF16), tab, seq)
            o = _moba(proj, batch, seq, width)
            x2 = _attn_out(o, proj, x2, attn_w_out[li].astype(BF16), post_g)
        else:
            c = conv_w_out.shape[1]
            nc = c // LANES
            wdw3 = jnp.pad(conv_w_dw[li], ((0, HALO_ROWS - CONV_K), (0, 0)))
            wdw3 = wdw3.reshape(HALO_ROWS, nc, LANES).transpose(1, 0, 2)
            bdw3 = conv_b_dw[li].reshape(nc, 1, LANES)
            u, sz = _conv_in(x2, pre_g, conv_w_in[li].astype(BF16), row(conv_b_in[li]))
            x2 = _conv_tail(u, sz, x2, wdw3, bdw3, row(conv_ln_g[li]), row(conv_ln_b[li]),
                            conv_w_out[li].astype(BF16), row(conv_b_out[li]), post_g, seq)
    return x2.reshape(batch, seq, d)
```

```python
import functools

import jax
import jax.numpy as jnp
from jax import lax
from jax.experimental import pallas as pl
from jax.experimental.pallas import tpu as pltpu

F32 = jnp.float32
BF16 = jnp.bfloat16

N_MIXERS = 2
N_HEADS = 16
MOBA_BLOCK = 256
MOBA_TOPK = 3
ROPE_THETA = 10000.0
CONV_K = 31
RMS_EPS = 1e-6
LN_EPS = 1e-5
NEG = -1e30
LOG2_E = 1.4426950408889634

LANES = 128
HALO_ROWS = 32
NORM_ROWS = 64
VMEM_LIMIT_BYTES = 56 * 1024 * 1024

_NT = (((1,), (1,)), ((), ()))


def _params(n_axes):
    return pltpu.CompilerParams(
        dimension_semantics=("arbitrary",) * n_axes,
        vmem_limit_bytes=VMEM_LIMIT_BYTES)


def _const_spec(shape):
    return pl.BlockSpec(shape, lambda *_: (0,) * len(shape), pipeline_mode=pl.Buffered(1))


def _rms_norm_rows(x_ref, g_ref, hn_ref):
    for r in range(x_ref.shape[0] // NORM_ROWS):
        rows = slice(r * NORM_ROWS, (r + 1) * NORM_ROWS)
        xf = x_ref[rows, :]
        ms = jnp.mean(xf * xf, axis=-1, keepdims=True)
        hn_ref[rows, :] = (xf * lax.rsqrt(ms + RMS_EPS) * g_ref[...]).astype(BF16)


def _silu(t):
    return t * jax.nn.sigmoid(t)


def _attn_in_kernel(x_ref, g_ref, w_ref, tab_ref, o_ref, hn_ref, *, rope_tiles):
    j = pl.program_id(1)

    def rope_dot():
        acc = jnp.dot(hn_ref[...], w_ref[...], preferred_element_type=F32)
        cos = tab_ref[0, 0]
        sin = tab_ref[0, 1]
        for h in range(acc.shape[1] // LANES):
            c = acc[:, h * LANES:(h + 1) * LANES]
            rot = pltpu.roll(c, LANES // 2, 1)
            o_ref[:, h * LANES:(h + 1) * LANES] = (c * cos + rot * sin).astype(BF16)

    @pl.when(j == 0)
    def _():
        _rms_norm_rows(x_ref, g_ref, hn_ref)
        rope_dot()

    @pl.when((j > 0) & (j < rope_tiles))
    def _():
        rope_dot()

    @pl.when(j >= rope_tiles)
    def _():
        o_ref[...] = jnp.dot(hn_ref[...], w_ref[...], preferred_element_type=F32).astype(BF16)


def _attn_in(x2, g, w, tab, seq, tm=1024, tn=1024):
    t, d = x2.shape
    n = w.shape[1]
    head_dim = tab.shape[-1]
    qk_tiles = (n // 4) // tn
    tiles_per_seq = seq // tm
    return pl.pallas_call(
        functools.partial(_attn_in_kernel, rope_tiles=2 * qk_tiles),
        grid=(t // tm, n // tn),
        in_specs=[
            pl.BlockSpec((tm, d), lambda i, j: (i, 0)),
            pl.BlockSpec((1, d), lambda i, j: (0, 0)),
            pl.BlockSpec((d, tn), lambda i, j: (0, j)),
            pl.BlockSpec((1, 2, tm, head_dim),
                         lambda i, j: (jnp.minimum(j // qk_tiles, 1), 0, i % tiles_per_seq, 0)),
        ],
        out_specs=pl.BlockSpec((tm, tn), lambda i, j: (i, j)),
        out_shape=jax.ShapeDtypeStruct((t, n), BF16),
        scratch_shapes=[pltpu.VMEM((tm, d), BF16)],
        compiler_params=_params(2),
        name="attn_in",
    )(x2, g, w, tab)


def _moba_kernel(q_ref, k_ref, v_ref, o_ref, vt_ref, *bufs, heads, lookahead):
    s_len = q_ref.shape[0]
    hd = q_ref.shape[1] // heads
    s_refs, p_refs = bufs[:lookahead + 1], bufs[lookahead + 1:]
    nb = s_len // MOBA_BLOCK
    blk = MOBA_BLOCK
    ext = vt_ref.shape[1]

    for h in range(heads):
        for n in range(nb):
            vb = v_ref[n * blk:(n + 1) * blk, h * hd:(h + 1) * hd].astype(F32)
            vt_ref[h, 0:hd, n * blk:(n + 1) * blk] = vb.T.astype(BF16)
        vt_ref[h, hd:ext, :] = jnp.ones((ext - hd, s_len), BF16)

    key_pos = lax.broadcasted_iota(jnp.int32, (blk, blk), 0)
    qry_pos = lax.broadcasted_iota(jnp.int32, (blk, blk), 1)
    causal = key_pos <= qry_pos
    blk_id = lax.broadcasted_iota(jnp.int32, (nb, blk), 0)

    def scores(u):
        h, j = units[u]
        cols = slice(h * hd, (h + 1) * hd)
        q_j = q_ref[j * blk:(j + 1) * blk, cols]
        s_all = lax.dot_general(k_ref[0:(j + 1) * blk, cols], q_j, _NT,
                                preferred_element_type=F32)
        maxes, sums = [], []
        for n in range(j + 1):
            rows = slice(n * blk, (n + 1) * blk)
            s_n = s_all[rows, :]
            if n == j:
                s_n = jnp.where(causal, s_n, NEG)
            else:
                sums.append(jnp.sum(s_n, axis=0, keepdims=True))
            maxes.append(jnp.max(s_n, axis=0, keepdims=True))
            s_refs[u % len(s_refs)][rows, :] = s_n
        m_run = maxes[j]
        if j == 0:
            return None, m_run
        pad = jnp.zeros((nb - j, blk), F32)
        gate = jnp.concatenate(sums + [pad], axis=0)
        beaten = jnp.zeros((nb, blk), jnp.int32)
        for m in range(j):
            g_m = gate[m:m + 1, :]
            beats = (g_m > gate) | ((g_m == gate) & (m < blk_id))
            beaten = beaten + beats.astype(jnp.int32)
        bias = jnp.where((beaten < MOBA_TOPK) & (blk_id < j), 0.0, NEG)
        for n in range(j):
            m_run = jnp.maximum(m_run, maxes[n] + bias[n:n + 1, :])
        return bias, m_run

    units = [(h, j) for h in range(heads) for j in range(nb)]
    pending = {u: scores(u) for u in range(min(lookahead, len(units)))}
    for u, (h, j) in enumerate(units):
        if u + lookahead < len(units):
            pending[u + lookahead] = scores(u + lookahead)
        bias, m_run = pending.pop(u)
        s_ref, p_ref = s_refs[u % len(s_refs)], p_refs[u % len(p_refs)]
        for n in range(j + 1):
            rows = slice(n * blk, (n + 1) * blk)
            shift = m_run if n == j else m_run - bias[n:n + 1, :]
            p_ref[rows, :] = jnp.exp2(s_ref[rows, :] - shift).astype(BF16)
        acc = jnp.dot(vt_ref[h, :, 0:(j + 1) * blk], p_ref[0:(j + 1) * blk, :],
                      preferred_element_type=F32)
        out_t = acc[0:hd, :] * (1.0 / acc[hd:hd + 1, :])
        o_ref[j * blk:(j + 1) * blk, h * hd:(h + 1) * hd] = out_t.T.astype(BF16)


def _moba(proj, batch, seq, width, heads=4, lookahead=2):
    hd = width // N_HEADS
    t = proj.shape[0]
    ext = hd + 16
    groups = N_HEADS // heads
    return pl.pallas_call(
        functools.partial(_moba_kernel, heads=heads, lookahead=lookahead),
        grid=(batch, groups),
        in_specs=[
            pl.BlockSpec((seq, heads * hd), lambda b, g: (b, g)),
            pl.BlockSpec((seq, heads * hd), lambda b, g: (b, groups + g)),
            pl.BlockSpec((seq, heads * hd), lambda b, g: (b, 2 * groups + g)),
        ],
        out_specs=pl.BlockSpec((seq, heads * hd), lambda b, g: (b, g)),
        out_shape=jax.ShapeDtypeStruct((t, width), BF16),
        scratch_shapes=([pltpu.VMEM((heads, ext, seq), BF16)]
                        + [pltpu.VMEM((seq, MOBA_BLOCK), F32)] * (lookahead + 1)
                        + [pltpu.VMEM((seq, MOBA_BLOCK), BF16)] * 2),
        compiler_params=_params(2),
        name="moba",
    )(proj, proj, proj)


def _post_norm_residual(y, x_ref, pg_ref, o_ref):
    ms = jnp.mean(y * y, axis=-1, keepdims=True)
    o_ref[...] = x_ref[...] + y * lax.rsqrt(ms + RMS_EPS) * pg_ref[...]


def _attn_out_kernel(o_in_ref, z_ref, x_ref, w_ref, pg_ref, o_ref):
    gated = (o_in_ref[...].astype(F32) * _silu(z_ref[...].astype(F32))).astype(BF16)
    y = jnp.dot(gated, w_ref[...], preferred_element_type=F32)
    _post_norm_residual(y, x_ref, pg_ref, o_ref)


def _attn_out(o, proj, x2, w, pg, tm=512):
    t, d = x2.shape
    width = o.shape[1]
    z_block = proj.shape[1] // width - 1
    return pl.pallas_call(
        _attn_out_kernel,
        grid=(t // tm,),
        in_specs=[
            pl.BlockSpec((tm, width), lambda i: (i, 0)),
            pl.BlockSpec((tm, width), lambda i: (i, z_block)),
            pl.BlockSpec((tm, d), lambda i: (i, 0)),
            _const_spec((width, d)),
            _const_spec((1, d)),
        ],
        out_specs=pl.BlockSpec((tm, d), lambda i: (i, 0)),
        out_shape=jax.ShapeDtypeStruct((t, d), F32),
        compiler_params=_params(1),
        name="attn_out",
    )(o, proj, x2, w, pg)


def _conv_in_kernel(x_ref, g_ref, wa_ref, wg_ref, wz_ref, ba_ref, bg_ref, u_ref, sz_ref, hn_ref):
    def glu_dots():
        hn = hn_ref[...]
        a = jnp.dot(hn, wa_ref[...], preferred_element_type=F32) + ba_ref[...]
        g = jnp.dot(hn, wg_ref[...], preferred_element_type=F32) + bg_ref[...]
        u_ref[...] = (a * jax.nn.sigmoid(g)).astype(BF16)
        z = jnp.dot(hn, wz_ref[...], preferred_element_type=F32)
        sz_ref[...] = _silu(z).astype(BF16)

    @pl.when(pl.program_id(1) == 0)
    def _():
        _rms_norm_rows(x_ref, g_ref, hn_ref)
        glu_dots()

    @pl.when(pl.program_id(1) != 0)
    def _():
        glu_dots()


def _conv_in(x2, g, w, b_in, tm=1024, tn=512):
    t, d = x2.shape
    c = w.shape[1] // 3
    nj = c // tn
    return pl.pallas_call(
        _conv_in_kernel,
        grid=(t // tm, nj),
        in_specs=[
            pl.BlockSpec((tm, d), lambda i, j: (i, 0)),
            pl.BlockSpec((1, d), lambda i, j: (0, 0)),
            pl.BlockSpec((d, tn), lambda i, j: (0, j)),
            pl.BlockSpec((d, tn), lambda i, j: (0, nj + j)),
            pl.BlockSpec((d, tn), lambda i, j: (0, 2 * nj + j)),
            pl.BlockSpec((1, tn), lambda i, j: (0, j)),
            pl.BlockSpec((1, tn), lambda i, j: (0, nj + j)),
        ],
        out_specs=[pl.BlockSpec((tm, tn), lambda i, j: (i, j)),
                   pl.BlockSpec((tm, tn), lambda i, j: (i, j))],
        out_shape=[jax.ShapeDtypeStruct((t, c), BF16), jax.ShapeDtypeStruct((t, c), BF16)],
        scratch_shapes=[pltpu.VMEM((tm, d), BF16)],
        compiler_params=_params(2),
        name="conv_in",
    )(x2, g, w, w, w, b_in, b_in)


def _conv_tail_kernel(u_ref, halo_ref, sz_ref, x_ref, wdw_ref, bdw_ref, lng_ref, lnb_ref,
                      w_ref, bo_ref, pg_ref, o_ref, hist_ref, conv_ref, gated_ref,
                      *, tiles_per_seq, row_chunk):
    tm, c = u_ref.shape
    nc = c // LANES

    first = (pl.program_id(0) % tiles_per_seq) == 0
    halo = jnp.where(first, 0.0, halo_ref[...].astype(F32))
    for ci in range(nc):
        hist_ref[ci, 0:HALO_ROWS, :] = halo[:, ci * LANES:(ci + 1) * LANES]
        hist_ref[ci, HALO_ROWS:HALO_ROWS + tm, :] = u_ref[:, ci * LANES:(ci + 1) * LANES].astype(F32)

    first_tap = HALO_ROWS - (CONV_K - 1)

    def lane_chunk(ci, carry):
        w = wdw_ref[ci]
        taps = [jnp.broadcast_to(w[k:k + 1, :], (row_chunk, LANES)) for k in range(CONV_K)]
        b0 = jnp.broadcast_to(bdw_ref[ci], (row_chunk, LANES))
        for r in range(tm // row_chunk):
            acc = b0
            for k in range(CONV_K):
                acc = acc + hist_ref[ci, pl.ds(r * row_chunk + first_tap + k, row_chunk), :] * taps[k]
            conv_ref[ci, pl.ds(r * row_chunk, row_chunk), :] = acc
        return carry
    lax.fori_loop(0, nc, lane_chunk, 0)

    inv_c = 1.0 / c
    s1 = conv_ref[0]
    for ci in range(1, nc):
        s1 = s1 + conv_ref[ci]
    mu = jnp.sum(s1, axis=1, keepdims=True) * inv_c
    s2 = jnp.zeros((tm, LANES), F32)
    for ci in range(nc):
        dlt = conv_ref[ci] - mu
        s2 = s2 + dlt * dlt
    rstd = lax.rsqrt(jnp.sum(s2, axis=1, keepdims=True) * inv_c + LN_EPS)
    for ci in range(nc):
        sl = slice(ci * LANES, (ci + 1) * LANES)
        y = (conv_ref[ci] - mu) * rstd * lng_ref[:, sl] + lnb_ref[:, sl]
        gated_ref[:, sl] = (_silu(y) * sz_ref[:, sl].astype(F32)).astype(BF16)

    y = jnp.dot(gated_ref[...], w_ref[...], preferred_element_type=F32) + bo_ref[...]
    _post_norm_residual(y, x_ref, pg_ref, o_ref)


def _conv_tail(u, sz, x2, wdw3, bdw3, ln_g, ln_b, w, b_out, pg, seq, tm=512, row_chunk=64):
    t, d = x2.shape
    c = u.shape[1]
    nc = c // LANES
    halo_per_tile = tm // HALO_ROWS
    return pl.pallas_call(
        functools.partial(_conv_tail_kernel, tiles_per_seq=seq // tm, row_chunk=row_chunk),
        grid=(t // tm,),
        in_specs=[
            pl.BlockSpec((tm, c), lambda i: (i, 0)),
            pl.BlockSpec((HALO_ROWS, c), lambda i: (jnp.maximum(i * halo_per_tile - 1, 0), 0)),
            pl.BlockSpec((tm, c), lambda i: (i, 0)),
            pl.BlockSpec((tm, d), lambda i: (i, 0)),
            _const_spec(wdw3.shape),
            _const_spec(bdw3.shape),
            _const_spec((1, c)),
            _const_spec((1, c)),
            _const_spec((c, d)),
            _const_spec((1, d)),
            _const_spec((1, d)),
        ],
        out_specs=pl.BlockSpec((tm, d), lambda i: (i, 0)),
        out_shape=jax.ShapeDtypeStruct((t, d), F32),
        scratch_shapes=[pltpu.VMEM((nc, HALO_ROWS + tm, LANES), F32),
                        pltpu.VMEM((nc, tm, LANES), F32),
                        pltpu.VMEM((tm, c), BF16)],
        compiler_params=_params(1),
        name="conv_tail",
    )(u, u, sz, x2, wdw3, bdw3, ln_g, ln_b, w, b_out, pg)


def _rope_tables(seq, head_dim):
    half = head_dim // 2
    inv_freq = 1.0 / (ROPE_THETA ** (jnp.arange(half, dtype=F32) * 2.0 / head_dim))
    ang = jnp.arange(seq, dtype=F32)[:, None] * inv_freq[None, :]
    cos, sin = jnp.cos(ang), jnp.sin(ang)
    k_tab = jnp.stack([jnp.concatenate([cos, cos], -1), jnp.concatenate([-sin, sin], -1)])
    return jnp.stack([k_tab * (head_dim ** -0.5 * LOG2_E), k_tab])


def kernel(x, pre_norm_g, post_norm_g, attn_w_in, attn_w_out, conv_w_in, conv_b_in, conv_w_dw,
           conv_b_dw, conv_ln_g, conv_ln_b, conv_w_out, conv_b_out):
    batch, seq, d = x.shape
    depth = pre_norm_g.shape[0]
    x2 = x.reshape(batch * seq, d)
    row = lambda v: v.reshape(1, -1)
    for i in range(depth):
        li = i // N_MIXERS
        pre_g, post_g = row(pre_norm_g[i]), row(post_norm_g[i])
        if i % N_MIXERS == 0:
            width = attn_w_out.shape[1]
            tab = _rope_tables(seq, width // N_HEADS)
            proj = _attn_in(x2, pre_g, attn_w_in[li].astype(BF16), tab, seq)
            o = _moba(proj, batch, seq, width)
            x2 = _attn_out(o, proj, x2, attn_w_out[li].astype(BF16), post_g)
        else:
            c = conv_w_out.shape[1]
            nc = c // LANES
            wdw3 = jnp.pad(conv_w_dw[li], ((0, HALO_ROWS - CONV_K), (0, 0)))
            wdw3 = wdw3.reshape(HALO_ROWS, nc, LANES).transpose(1, 0, 2)
            bdw3 = conv_b_dw[li].reshape(nc, 1, LANES)
            u, sz = _conv_in(x2, pre_g, conv_w_in[li].astype(BF16), row(conv_b_in[li]))
            x2 = _conv_tail(u, sz, x2, wdw3, bdw3, row(conv_ln_g[li]), row(conv_ln_b[li]),
                            conv_w_out[li].astype(BF16), row(conv_b_out[li]), post_g, seq)
    return x2.reshape(batch, seq, d)
```

```python
import functools

import jax
import jax.numpy as jnp
from jax import lax
from jax.experimental import pallas as pl
from jax.experimental.pallas import tpu as pltpu

F32 = jnp.float32
BF16 = jnp.bfloat16

N_MIXERS = 2
N_HEADS = 16
MOBA_BLOCK = 256
MOBA_TOPK = 3
ROPE_THETA = 10000.0
CONV_K = 31
RMS_EPS = 1e-6
LN_EPS = 1e-5
NEG = -1e30
LOG2_E = 1.4426950408889634

LANES = 128
HALO_ROWS = 32
PAD_ROWS = 16
TAP_GROUP = 8
NORM_ROWS = 64
VMEM_LIMIT_BYTES = 56 * 1024 * 1024

_NT = (((1,), (1,)), ((), ()))


def _params(n_axes):
    return pltpu.CompilerParams(
        dimension_semantics=("arbitrary",) * n_axes,
        vmem_limit_bytes=VMEM_LIMIT_BYTES)


def _const_spec(shape):
    return pl.BlockSpec(shape, lambda *_: (0,) * len(shape), pipeline_mode=pl.Buffered(1))


def _rms_norm_rows(x_ref, g_ref, hn_ref):
    for r in range(x_ref.shape[0] // NORM_ROWS):
        rows = slice(r * NORM_ROWS, (r + 1) * NORM_ROWS)
        xf = x_ref[rows, :]
        ms = jnp.mean(xf * xf, axis=-1, keepdims=True)
        hn_ref[rows, :] = (xf * lax.rsqrt(ms + RMS_EPS) * g_ref[...]).astype(BF16)


def _silu(t):
    return t * jax.nn.sigmoid(t)


def _attn_in_kernel(x_ref, g_ref, w_ref, tab_ref, o_ref, hn_ref, *, rope_tiles):
    j = pl.program_id(1)

    def rope_dot():
        acc = jnp.dot(hn_ref[...], w_ref[...], preferred_element_type=F32)
        cos = tab_ref[0, 0]
        sin = tab_ref[0, 1]
        for h in range(acc.shape[1] // LANES):
            c = acc[:, h * LANES:(h + 1) * LANES]
            rot = pltpu.roll(c, LANES // 2, 1)
            o_ref[:, h * LANES:(h + 1) * LANES] = (c * cos + rot * sin).astype(BF16)

    @pl.when(j == 0)
    def _():
        _rms_norm_rows(x_ref, g_ref, hn_ref)
        rope_dot()

    @pl.when((j > 0) & (j < rope_tiles))
    def _():
        rope_dot()

    @pl.when(j >= rope_tiles)
    def _():
        o_ref[...] = jnp.dot(hn_ref[...], w_ref[...], preferred_element_type=F32).astype(BF16)


def _attn_in(x2, g, w, tab, seq, tm=1024, tn=1024):
    t, d = x2.shape
    n = w.shape[1]
    head_dim = tab.shape[-1]
    qk_tiles = (n // 4) // tn
    tiles_per_seq = seq // tm
    return pl.pallas_call(
        functools.partial(_attn_in_kernel, rope_tiles=2 * qk_tiles),
        grid=(t // tm, n // tn),
        in_specs=[
            pl.BlockSpec((tm, d), lambda i, j: (i, 0)),
            pl.BlockSpec((1, d), lambda i, j: (0, 0)),
            pl.BlockSpec((d, tn), lambda i, j: (0, j)),
            pl.BlockSpec((1, 2, tm, head_dim),
                         lambda i, j: (jnp.minimum(j // qk_tiles, 1), 0, i % tiles_per_seq, 0)),
        ],
        out_specs=pl.BlockSpec((tm, tn), lambda i, j: (i, j)),
        out_shape=jax.ShapeDtypeStruct((t, n), BF16),
        scratch_shapes=[pltpu.VMEM((tm, d), BF16)],
        compiler_params=_params(2),
        name="attn_in",
    )(x2, g, w, tab)


def _moba_kernel(q_ref, k_ref, v_ref, o_ref, vt_ref, *bufs, heads, lookahead):
    s_len = q_ref.shape[0]
    hd = q_ref.shape[1] // heads
    s_refs, p_refs = bufs[:lookahead + 1], bufs[lookahead + 1:]
    nb = s_len // MOBA_BLOCK
    blk = MOBA_BLOCK
    ext = vt_ref.shape[1]

    for h in range(heads):
        for n in range(nb):
            vb = v_ref[n * blk:(n + 1) * blk, h * hd:(h + 1) * hd].astype(F32)
            vt_ref[h, 0:hd, n * blk:(n + 1) * blk] = vb.T.astype(BF16)
        vt_ref[h, hd:ext, :] = jnp.ones((ext - hd, s_len), BF16)

    key_pos = lax.broadcasted_iota(jnp.int32, (blk, blk), 0)
    qry_pos = lax.broadcasted_iota(jnp.int32, (blk, blk), 1)
    causal = key_pos <= qry_pos
    blk_id = lax.broadcasted_iota(jnp.int32, (nb, blk), 0)

    def scores(u):
        h, j = units[u]
        cols = slice(h * hd, (h + 1) * hd)
        q_j = q_ref[j * blk:(j + 1) * blk, cols]
        s_all = lax.dot_general(k_ref[0:(j + 1) * blk, cols], q_j, _NT,
                                preferred_element_type=F32)
        maxes, sums = [], []
        for n in range(j + 1):
            rows = slice(n * blk, (n + 1) * blk)
            s_n = s_all[rows, :]
            if n == j:
                s_n = jnp.where(causal, s_n, NEG)
            else:
                sums.append(jnp.sum(s_n, axis=0, keepdims=True))
            maxes.append(jnp.max(s_n, axis=0, keepdims=True))
            s_refs[u % len(s_refs)][rows, :] = s_n
        m_run = maxes[j]
        if j == 0:
            return None, m_run
        pad = jnp.zeros((nb - j, blk), F32)
        gate = jnp.concatenate(sums + [pad], axis=0)
        beaten = jnp.zeros((nb, blk), jnp.int32)
        for m in range(j):
            g_m = gate[m:m + 1, :]
            beats = (g_m > gate) | ((g_m == gate) & (m < blk_id))
            beaten = beaten + beats.astype(jnp.int32)
        bias = jnp.where((beaten < MOBA_TOPK) & (blk_id < j), 0.0, NEG)
        for n in range(j):
            m_run = jnp.maximum(m_run, maxes[n] + bias[n:n + 1, :])
        return bias, m_run

    units = [(h, j) for h in range(heads) for j in range(nb)]
    pending = {u: scores(u) for u in range(min(lookahead, len(units)))}
    for u, (h, j) in enumerate(units):
        if u + lookahead < len(units):
            pending[u + lookahead] = scores(u + lookahead)
        bias, m_run = pending.pop(u)
        s_ref, p_ref = s_refs[u % len(s_refs)], p_refs[u % len(p_refs)]
        for n in range(j + 1):
            rows = slice(n * blk, (n + 1) * blk)
            shift = m_run if n == j else m_run - bias[n:n + 1, :]
            p_ref[rows, :] = jnp.exp2(s_ref[rows, :] - shift).astype(BF16)
        acc = jnp.dot(vt_ref[h, :, 0:(j + 1) * blk], p_ref[0:(j + 1) * blk, :],
                      preferred_element_type=F32)
        out_t = acc[0:hd, :] * (1.0 / acc[hd:hd + 1, :])
        o_ref[j * blk:(j + 1) * blk, h * hd:(h + 1) * hd] = out_t.T.astype(BF16)


def _moba(proj, batch, seq, width, heads=4, lookahead=2):
    hd = width // N_HEADS
    t = proj.shape[0]
    ext = hd + 16
    groups = N_HEADS // heads
    return pl.pallas_call(
        functools.partial(_moba_kernel, heads=heads, lookahead=lookahead),
        grid=(batch, groups),
        in_specs=[
            pl.BlockSpec((seq, heads * hd), lambda b, g: (b, g)),
            pl.BlockSpec((seq, heads * hd), lambda b, g: (b, groups + g)),
            pl.BlockSpec((seq, heads * hd), lambda b, g: (b, 2 * groups + g)),
        ],
        out_specs=pl.BlockSpec((seq, heads * hd), lambda b, g: (b, g)),
        out_shape=jax.ShapeDtypeStruct((t, width), BF16),
        scratch_shapes=([pltpu.VMEM((heads, ext, seq), BF16)]
                        + [pltpu.VMEM((seq, MOBA_BLOCK), F32)] * (lookahead + 1)
                        + [pltpu.VMEM((seq, MOBA_BLOCK), BF16)] * 2),
        compiler_params=_params(2),
        name="moba",
    )(proj, proj, proj)


def _post_norm_residual(y, x_ref, pg_ref, o_ref):
    ms = jnp.mean(y * y, axis=-1, keepdims=True)
    o_ref[...] = x_ref[...] + y * lax.rsqrt(ms + RMS_EPS) * pg_ref[...]


def _attn_out_kernel(o_in_ref, z_ref, x_ref, w_ref, pg_ref, o_ref):
    gated = (o_in_ref[...].astype(F32) * _silu(z_ref[...].astype(F32))).astype(BF16)
    y = jnp.dot(gated, w_ref[...], preferred_element_type=F32)
    _post_norm_residual(y, x_ref, pg_ref, o_ref)


def _attn_out(o, proj, x2, w, pg, tm=512):
    t, d = x2.shape
    width = o.shape[1]
    z_block = proj.shape[1] // width - 1
    return pl.pallas_call(
        _attn_out_kernel,
        grid=(t // tm,),
        in_specs=[
            pl.BlockSpec((tm, width), lambda i: (i, 0)),
            pl.BlockSpec((tm, width), lambda i: (i, z_block)),
            pl.BlockSpec((tm, d), lambda i: (i, 0)),
            _const_spec((width, d)),
            _const_spec((1, d)),
        ],
        out_specs=pl.BlockSpec((tm, d), lambda i: (i, 0)),
        out_shape=jax.ShapeDtypeStruct((t, d), F32),
        compiler_params=_params(1),
        name="attn_out",
    )(o, proj, x2, w, pg)


def _conv_in_kernel(x_ref, g_ref, wa_ref, wg_ref, wz_ref, ba_ref, bg_ref, u_ref, sz_ref, hn_ref):
    def glu_dots():
        hn = hn_ref[...]
        a = jnp.dot(hn, wa_ref[...], preferred_element_type=F32) + ba_ref[...]
        g = jnp.dot(hn, wg_ref[...], preferred_element_type=F32) + bg_ref[...]
        u_ref[...] = (a * jax.nn.sigmoid(g)).astype(BF16)
        z = jnp.dot(hn, wz_ref[...], preferred_element_type=F32)
        sz_ref[...] = _silu(z).astype(BF16)

    @pl.when(pl.program_id(1) == 0)
    def _():
        _rms_norm_rows(x_ref, g_ref, hn_ref)
        glu_dots()

    @pl.when(pl.program_id(1) != 0)
    def _():
        glu_dots()


def _conv_in(x2, g, w, b_in, tm=1024, tn=512):
    t, d = x2.shape
    c = w.shape[1] // 3
    nj = c // tn
    return pl.pallas_call(
        _conv_in_kernel,
        grid=(t // tm, nj),
        in_specs=[
            pl.BlockSpec((tm, d), lambda i, j: (i, 0)),
            pl.BlockSpec((1, d), lambda i, j: (0, 0)),
            pl.BlockSpec((d, tn), lambda i, j: (0, j)),
            pl.BlockSpec((d, tn), lambda i, j: (0, nj + j)),
            pl.BlockSpec((d, tn), lambda i, j: (0, 2 * nj + j)),
            pl.BlockSpec((1, tn), lambda i, j: (0, j)),
            pl.BlockSpec((1, tn), lambda i, j: (0, nj + j)),
        ],
        out_specs=[pl.BlockSpec((tm, tn), lambda i, j: (i, j)),
                   pl.BlockSpec((tm, tn), lambda i, j: (i, j))],
        out_shape=[jax.ShapeDtypeStruct((t, c), BF16), jax.ShapeDtypeStruct((t, c), BF16)],
        scratch_shapes=[pltpu.VMEM((tm, d), BF16)],
        compiler_params=_params(2),
        name="conv_in",
    )(x2, g, w, w, w, b_in, b_in)


def _conv_tail_kernel(u_ref, halo_ref, sz_ref, x_ref, wdw_ref, bdw_ref, lng_ref, lnb_ref,
                      w_ref, bo_ref, pg_ref, o_ref, hist_ref, conv_ref, gated_ref, pa_ref, pb_ref,
                      *, tiles_per_seq, pair_chunk):
    tm, c = u_ref.shape
    nc = c // LANES

    first = (pl.program_id(0) % tiles_per_seq) == 0
    halo = jnp.where(first, 0.0, halo_ref[...].astype(F32))
    for ci in range(nc):
        hist_ref[ci, 0:HALO_ROWS, :] = halo[:, ci * LANES:(ci + 1) * LANES]
        hist_ref[ci, HALO_ROWS:HALO_ROWS + tm, :] = u_ref[:, ci * LANES:(ci + 1) * LANES].astype(F32)

    half = pa_ref.shape[0]
    hi_mask = jnp.uint32(0xFFFF0000)
    first_tap = HALO_ROWS - (CONV_K - 1)

    def lane_chunk(ci, carry):
        hist_ref[ci, HALO_ROWS + tm:, :] = jnp.zeros((hist_ref.shape[1] - HALO_ROWS - tm, LANES), F32)
        even = pltpu.bitcast(hist_ref[ci, pl.ds(0, half, stride=2), :], jnp.uint32)
        odd = pltpu.bitcast(hist_ref[ci, pl.ds(1, half, stride=2), :], jnp.uint32)
        even_next = pltpu.bitcast(hist_ref[ci, pl.ds(2, half, stride=2), :], jnp.uint32)
        pa_ref[...] = (even >> 16) | (odd & hi_mask)
        pb_ref[...] = (odd >> 16) | (even_next & hi_mask)

        w = wdw_ref[ci].astype(BF16)
        taps = [jnp.broadcast_to(w[k:k + 1, :], (2 * pair_chunk, LANES)) for k in range(CONV_K)]
        b0 = jnp.broadcast_to(bdw_ref[ci], (pair_chunk, LANES))
        for r in range(tm // 2 // pair_chunk):
            base = r * pair_chunk
            acc_even, acc_odd = b0, b0
            for g0 in range(0, CONV_K, TAP_GROUP):
                prods = []
                for k in range(g0, min(g0 + TAP_GROUP, CONV_K)):
                    off = first_tap + k
                    src = pa_ref if off % 2 == 0 else pb_ref
                    words = src[pl.ds(base + off // 2, pair_chunk), :]
                    prods.append(pltpu.bitcast(words, BF16) * taps[k])
                while len(prods) > 1:
                    prods = [prods[i] + prods[i + 1] if i + 1 < len(prods) else prods[i]
                             for i in range(0, len(prods), 2)]
                gw = pltpu.bitcast(prods[0], jnp.uint32)
                acc_even = acc_even + pltpu.bitcast(gw << 16, F32)
                acc_odd = acc_odd + pltpu.bitcast(gw & hi_mask, F32)
            conv_ref[ci, pl.ds(2 * base, pair_chunk, stride=2), :] = acc_even
            conv_ref[ci, pl.ds(2 * base + 1, pair_chunk, stride=2), :] = acc_odd
        return carry
    lax.fori_loop(0, nc, lane_chunk, 0)

    inv_c = 1.0 / c
    s1 = conv_ref[0]
    for ci in range(1, nc):
        s1 = s1 + conv_ref[ci]
    mu = jnp.sum(s1, axis=1, keepdims=True) * inv_c
    s2 = jnp.zeros((tm, LANES), F32)
    for ci in range(nc):
        dlt = conv_ref[ci] - mu
        s2 = s2 + dlt * dlt
    rstd = lax.rsqrt(jnp.sum(s2, axis=1, keepdims=True) * inv_c + LN_EPS)
    for ci in range(nc):
        sl = slice(ci * LANES, (ci + 1) * LANES)
        y = (conv_ref[ci] - mu) * rstd * lng_ref[:, sl] + lnb_ref[:, sl]
        gated_ref[:, sl] = (_silu(y) * sz_ref[:, sl].astype(F32)).astype(BF16)

    y = jnp.dot(gated_ref[...], w_ref[...], preferred_element_type=F32) + bo_ref[...]
    _post_norm_residual(y, x_ref, pg_ref, o_ref)


def _conv_tail(u, sz, x2, wdw3, bdw3, ln_g, ln_b, w, b_out, pg, seq, tm=512, pair_chunk=32):
    t, d = x2.shape
    c = u.shape[1]
    nc = c // LANES
    halo_per_tile = tm // HALO_ROWS
    return pl.pallas_call(
        functools.partial(_conv_tail_kernel, tiles_per_seq=seq // tm, pair_chunk=pair_chunk),
        grid=(t // tm,),
        in_specs=[
            pl.BlockSpec((tm, c), lambda i: (i, 0)),
            pl.BlockSpec((HALO_ROWS, c), lambda i: (jnp.maximum(i * halo_per_tile - 1, 0), 0)),
            pl.BlockSpec((tm, c), lambda i: (i, 0)),
            pl.BlockSpec((tm, d), lambda i: (i, 0)),
            _const_spec(wdw3.shape),
            _const_spec(bdw3.shape),
            _const_spec((1, c)),
            _const_spec((1, c)),
            _const_spec((c, d)),
            _const_spec((1, d)),
            _const_spec((1, d)),
        ],
        out_specs=pl.BlockSpec((tm, d), lambda i: (i, 0)),
        out_shape=jax.ShapeDtypeStruct((t, d), F32),
        scratch_shapes=[pltpu.VMEM((nc, HALO_ROWS + tm + PAD_ROWS, LANES), F32),
                        pltpu.VMEM((nc, tm, LANES), F32),
                        pltpu.VMEM((tm, c), BF16),
                        pltpu.VMEM(((HALO_ROWS + tm) // 2, LANES), jnp.uint32),
                        pltpu.VMEM(((HALO_ROWS + tm) // 2, LANES), jnp.uint32)],
        compiler_params=_params(1),
        name="conv_tail",
    )(u, u, sz, x2, wdw3, bdw3, ln_g, ln_b, w, b_out, pg)


def _rope_tables(seq, head_dim):
    half = head_dim // 2
    inv_freq = 1.0 / (ROPE_THETA ** (jnp.arange(half, dtype=F32) * 2.0 / head_dim))
    ang = jnp.arange(seq, dtype=F32)[:, None] * inv_freq[None, :]
    cos, sin = jnp.cos(ang), jnp.sin(ang)
    k_tab = jnp.stack([jnp.concatenate([cos, cos], -1), jnp.concatenate([-sin, sin], -1)])
    return jnp.stack([k_tab * (head_dim ** -0.5 * LOG2_E), k_tab])


def kernel(x, pre_norm_g, post_norm_g, attn_w_in, attn_w_out, conv_w_in, conv_b_in, conv_w_dw,
           conv_b_dw, conv_ln_g, conv_ln_b, conv_w_out, conv_b_out):
    batch, seq, d = x.shape
    depth = pre_norm_g.shape[0]
    x2 = x.reshape(batch * seq, d)
    row = lambda v: v.reshape(1, -1)
    for i in range(depth):
        li = i // N_MIXERS
        pre_g, post_g = row(pre_norm_g[i]), row(post_norm_g[i])
        if i % N_MIXERS == 0:
            width = attn_w_out.shape[1]
            tab = _rope_tables(seq, width // N_HEADS)
            proj = _attn_in(x2, pre_g, attn_w_in[li].astype(BF16), tab, seq)
            o = _moba(proj, batch, seq, width)
            x2 = _attn_out(o, proj, x2, attn_w_out[li].astype(BF16), post_g)
        else:
            c = conv_w_out.shape[1]
            nc = c // LANES
            wdw3 = jnp.pad(conv_w_dw[li], ((0, HALO_ROWS - CONV_K), (0, 0)))
            wdw3 = wdw3.reshape(HALO_ROWS, nc, LANES).transpose(1, 0, 2)
            bdw3 = conv_b_dw[li].reshape(nc, 1, LANES)
            u, sz = _conv_in(x2, pre_g, conv_w_in[li].astype(BF16), row(conv_b_in[li]))
            x2 = _conv_tail(u, sz, x2, wdw3, bdw3, row(conv_ln_g[li]), row(conv_ln_b[li]),
                            conv_w_out[li].astype(BF16), row(conv_b_out[li]), post_g, seq)
    return x2.reshape(batch, seq, d)
```

```python
import functools

import jax
import jax.numpy as jnp
from jax import lax
from jax.experimental import pallas as pl
from jax.experimental.pallas import tpu as pltpu

F32 = jnp.float32
BF16 = jnp.bfloat16

N_MIXERS = 2
N_HEADS = 16
MOBA_BLOCK = 256
MOBA_TOPK = 3
ROPE_THETA = 10000.0
CONV_K = 31
RMS_EPS = 1e-6
LN_EPS = 1e-5
NEG = -1e30
LOG2_E = 1.4426950408889634

LANES = 128
HALO_ROWS = 32
PAD_ROWS = 16
TAP_GROUP = 8
CAST_ROWS = 256
NORM_ROWS = 64
VMEM_LIMIT_BYTES = 56 * 1024 * 1024

_NT = (((1,), (1,)), ((), ()))


def _params(n_axes):
    return pltpu.CompilerParams(
        dimension_semantics=("arbitrary",) * n_axes,
        vmem_limit_bytes=VMEM_LIMIT_BYTES)


def _const_spec(shape):
    return pl.BlockSpec(shape, lambda *_: (0,) * len(shape), pipeline_mode=pl.Buffered(1))


def _rms_norm_rows(x_ref, g_ref, hn_ref):
    for r in range(x_ref.shape[0] // NORM_ROWS):
        rows = slice(r * NORM_ROWS, (r + 1) * NORM_ROWS)
        xf = x_ref[rows, :]
        ms = jnp.mean(xf * xf, axis=-1, keepdims=True)
        hn_ref[rows, :] = (xf * lax.rsqrt(ms + RMS_EPS) * g_ref[...]).astype(BF16)


def _silu(t):
    return t * jax.nn.sigmoid(t)


def _cast_rows(src_ref, dst_ref):
    for r in range(src_ref.shape[0] // CAST_ROWS):
        rows = slice(r * CAST_ROWS, (r + 1) * CAST_ROWS)
        dst_ref[rows, :] = src_ref[rows, :].astype(BF16)


def _attn_in_kernel(x_ref, g_ref, w_ref, tab_ref, o_ref, hn_ref, wb_ref, *, rope_tiles):
    j = pl.program_id(1)

    def rope_dot():
        _cast_rows(w_ref, wb_ref)
        acc = jnp.dot(hn_ref[...], wb_ref[...], preferred_element_type=F32)
        cos = tab_ref[0, 0]
        sin = tab_ref[0, 1]
        for h in range(acc.shape[1] // LANES):
            c = acc[:, h * LANES:(h + 1) * LANES]
            rot = pltpu.roll(c, LANES // 2, 1)
            o_ref[:, h * LANES:(h + 1) * LANES] = (c * cos + rot * sin).astype(BF16)

    @pl.when(j == 0)
    def _():
        _rms_norm_rows(x_ref, g_ref, hn_ref)
        rope_dot()

    @pl.when((j > 0) & (j < rope_tiles))
    def _():
        rope_dot()

    @pl.when(j >= rope_tiles)
    def _():
        _cast_rows(w_ref, wb_ref)
        o_ref[...] = jnp.dot(hn_ref[...], wb_ref[...], preferred_element_type=F32).astype(BF16)


def _attn_in(x2, g, w, tab, seq, tm=1024, tn=1024):
    t, d = x2.shape
    n = w.shape[1]
    head_dim = tab.shape[-1]
    qk_tiles = (n // 4) // tn
    tiles_per_seq = seq // tm
    return pl.pallas_call(
        functools.partial(_attn_in_kernel, rope_tiles=2 * qk_tiles),
        grid=(t // tm, n // tn),
        in_specs=[
            pl.BlockSpec((tm, d), lambda i, j: (i, 0)),
            pl.BlockSpec((1, d), lambda i, j: (0, 0)),
            pl.BlockSpec((d, tn), lambda i, j: (0, j)),
            pl.BlockSpec((1, 2, tm, head_dim),
                         lambda i, j: (jnp.minimum(j // qk_tiles, 1), 0, i % tiles_per_seq, 0)),
        ],
        out_specs=pl.BlockSpec((tm, tn), lambda i, j: (i, j)),
        out_shape=jax.ShapeDtypeStruct((t, n), BF16),
        scratch_shapes=[pltpu.VMEM((tm, d), BF16), pltpu.VMEM((d, tn), BF16)],
        compiler_params=_params(2),
        name="attn_in",
    )(x2, g, w, tab)


def _moba_kernel(q_ref, k_ref, v_ref, o_ref, vt_ref, *bufs, heads, lookahead):
    s_len = q_ref.shape[0]
    hd = q_ref.shape[1] // heads
    s_refs, p_refs = bufs[:lookahead + 1], bufs[lookahead + 1:]
    nb = s_len // MOBA_BLOCK
    blk = MOBA_BLOCK
    ext = vt_ref.shape[1]

    for h in range(heads):
        for n in range(nb):
            vb = v_ref[n * blk:(n + 1) * blk, h * hd:(h + 1) * hd].astype(F32)
            vt_ref[h, 0:hd, n * blk:(n + 1) * blk] = vb.T.astype(BF16)
        vt_ref[h, hd:ext, :] = jnp.ones((ext - hd, s_len), BF16)

    key_pos = lax.broadcasted_iota(jnp.int32, (blk, blk), 0)
    qry_pos = lax.broadcasted_iota(jnp.int32, (blk, blk), 1)
    causal = key_pos <= qry_pos
    blk_id = lax.broadcasted_iota(jnp.int32, (nb, blk), 0)

    def scores(u):
        h, j = units[u]
        cols = slice(h * hd, (h + 1) * hd)
        q_j = q_ref[j * blk:(j + 1) * blk, cols]
        s_all = lax.dot_general(k_ref[0:(j + 1) * blk, cols], q_j, _NT,
                                preferred_element_type=F32)
        maxes, sums = [], []
        for n in range(j + 1):
            rows = slice(n * blk, (n + 1) * blk)
            s_n = s_all[rows, :]
            if n == j:
                s_n = jnp.where(causal, s_n, NEG)
            else:
                sums.append(jnp.sum(s_n, axis=0, keepdims=True))
            maxes.append(jnp.max(s_n, axis=0, keepdims=True))
            s_refs[u % len(s_refs)][rows, :] = s_n
        m_run = maxes[j]
        if j == 0:
            return None, m_run
        pad = jnp.zeros((nb - j, blk), F32)
        gate = jnp.concatenate(sums + [pad], axis=0)
        beaten = jnp.zeros((nb, blk), jnp.int32)
        for m in range(j):
            g_m = gate[m:m + 1, :]
            beats = (g_m > gate) | ((g_m == gate) & (m < blk_id))
            beaten = beaten + beats.astype(jnp.int32)
        bias = jnp.where((beaten < MOBA_TOPK) & (blk_id < j), 0.0, NEG)
        for n in range(j):
            m_run = jnp.maximum(m_run, maxes[n] + bias[n:n + 1, :])
        return bias, m_run

    units = [(h, j) for h in range(heads) for j in range(nb)]
    pending = {u: scores(u) for u in range(min(lookahead, len(units)))}
    for u, (h, j) in enumerate(units):
        if u + lookahead < len(units):
            pending[u + lookahead] = scores(u + lookahead)
        bias, m_run = pending.pop(u)
        s_ref, p_ref = s_refs[u % len(s_refs)], p_refs[u % len(p_refs)]
        for n in range(j + 1):
            rows = slice(n * blk, (n + 1) * blk)
            shift = m_run if n == j else m_run - bias[n:n + 1, :]
            p_ref[rows, :] = jnp.exp2(s_ref[rows, :] - shift).astype(BF16)
        acc = jnp.dot(vt_ref[h, :, 0:(j + 1) * blk], p_ref[0:(j + 1) * blk, :],
                      preferred_element_type=F32)
        out_t = acc[0:hd, :] * (1.0 / acc[hd:hd + 1, :])
        o_ref[j * blk:(j + 1) * blk, h * hd:(h + 1) * hd] = out_t.T.astype(BF16)


def _moba(proj, batch, seq, width, heads=4, lookahead=2):
    hd = width // N_HEADS
    t = proj.shape[0]
    ext = hd + 16
    groups = N_HEADS // heads
    return pl.pallas_call(
        functools.partial(_moba_kernel, heads=heads, lookahead=lookahead),
        grid=(batch, groups),
        in_specs=[
            pl.BlockSpec((seq, heads * hd), lambda b, g: (b, g)),
            pl.BlockSpec((seq, heads * hd), lambda b, g: (b, groups + g)),
            pl.BlockSpec((seq, heads * hd), lambda b, g: (b, 2 * groups + g)),
        ],
        out_specs=pl.BlockSpec((seq, heads * hd), lambda b, g: (b, g)),
        out_shape=jax.ShapeDtypeStruct((t, width), BF16),
        scratch_shapes=([pltpu.VMEM((heads, ext, seq), BF16)]
                        + [pltpu.VMEM((seq, MOBA_BLOCK), F32)] * (lookahead + 1)
                        + [pltpu.VMEM((seq, MOBA_BLOCK), BF16)] * 2),
        compiler_params=_params(2),
        name="moba",
    )(proj, proj, proj)


def _post_norm_residual(y, x_ref, pg_ref, o_ref):
    ms = jnp.mean(y * y, axis=-1, keepdims=True)
    o_ref[...] = x_ref[...] + y * lax.rsqrt(ms + RMS_EPS) * pg_ref[...]


def _attn_out_kernel(o_in_ref, z_ref, x_ref, w_ref, pg_ref, o_ref):
    gated = (o_in_ref[...].astype(F32) * _silu(z_ref[...].astype(F32))).astype(BF16)
    y = jnp.dot(gated, w_ref[...], preferred_element_type=F32)
    _post_norm_residual(y, x_ref, pg_ref, o_ref)


def _attn_out(o, proj, x2, w, pg, tm=512):
    t, d = x2.shape
    width = o.shape[1]
    z_block = proj.shape[1] // width - 1
    return pl.pallas_call(
        _attn_out_kernel,
        grid=(t // tm,),
        in_specs=[
            pl.BlockSpec((tm, width), lambda i: (i, 0)),
            pl.BlockSpec((tm, width), lambda i: (i, z_block)),
            pl.BlockSpec((tm, d), lambda i: (i, 0)),
            _const_spec((width, d)),
            _const_spec((1, d)),
        ],
        out_specs=pl.BlockSpec((tm, d), lambda i: (i, 0)),
        out_shape=jax.ShapeDtypeStruct((t, d), F32),
        compiler_params=_params(1),
        name="attn_out",
    )(o, proj, x2, w, pg)


def _conv_in_kernel(x_ref, g_ref, wa_ref, wg_ref, wz_ref, ba_ref, bg_ref, u_ref, sz_ref, hn_ref):
    def glu_dots():
        hn = hn_ref[...]
        a = jnp.dot(hn, wa_ref[...], preferred_element_type=F32) + ba_ref[...]
        g = jnp.dot(hn, wg_ref[...], preferred_element_type=F32) + bg_ref[...]
        u_ref[...] = (a * jax.nn.sigmoid(g)).astype(BF16)
        z = jnp.dot(hn, wz_ref[...], preferred_element_type=F32)
        sz_ref[...] = _silu(z).astype(BF16)

    @pl.when(pl.program_id(1) == 0)
    def _():
        _rms_norm_rows(x_ref, g_ref, hn_ref)
        glu_dots()

    @pl.when(pl.program_id(1) != 0)
    def _():
        glu_dots()


def _conv_in(x2, g, w, b_in, tm=1024, tn=512):
    t, d = x2.shape
    c = w.shape[1] // 3
    nj = c // tn
    return pl.pallas_call(
        _conv_in_kernel,
        grid=(t // tm, nj),
        in_specs=[
            pl.BlockSpec((tm, d), lambda i, j: (i, 0)),
            pl.BlockSpec((1, d), lambda i, j: (0, 0)),
            pl.BlockSpec((d, tn), lambda i, j: (0, j)),
            pl.BlockSpec((d, tn), lambda i, j: (0, nj + j)),
            pl.BlockSpec((d, tn), lambda i, j: (0, 2 * nj + j)),
            pl.BlockSpec((1, tn), lambda i, j: (0, j)),
            pl.BlockSpec((1, tn), lambda i, j: (0, nj + j)),
        ],
        out_specs=[pl.BlockSpec((tm, tn), lambda i, j: (i, j)),
                   pl.BlockSpec((tm, tn), lambda i, j: (i, j))],
        out_shape=[jax.ShapeDtypeStruct((t, c), BF16), jax.ShapeDtypeStruct((t, c), BF16)],
        scratch_shapes=[pltpu.VMEM((tm, d), BF16)],
        compiler_params=_params(2),
        name="conv_in",
    )(x2, g, w, w, w, b_in, b_in)


def _conv_tail_kernel(u_ref, halo_ref, sz_ref, x_ref, wdw_ref, bdw_ref, lng_ref, lnb_ref,
                      w_ref, bo_ref, pg_ref, o_ref, hist_ref, conv_ref, gated_ref, pa_ref, pb_ref,
                      *, tiles_per_seq, pair_chunk):
    tm, c = u_ref.shape
    nc = c // LANES

    first = (pl.program_id(0) % tiles_per_seq) == 0
    halo = jnp.where(first, 0.0, halo_ref[...].astype(F32))
    for ci in range(nc):
        hist_ref[ci, 0:HALO_ROWS, :] = halo[:, ci * LANES:(ci + 1) * LANES]
        hist_ref[ci, HALO_ROWS:HALO_ROWS + tm, :] = u_ref[:, ci * LANES:(ci + 1) * LANES].astype(F32)

    half = pa_ref.shape[0]
    hi_mask = jnp.uint32(0xFFFF0000)
    first_tap = HALO_ROWS - (CONV_K - 1)

    def lane_chunk(ci, carry):
        hist_ref[ci, HALO_ROWS + tm:, :] = jnp.zeros((hist_ref.shape[1] - HALO_ROWS - tm, LANES), F32)
        even = pltpu.bitcast(hist_ref[ci, pl.ds(0, half, stride=2), :], jnp.uint32)
        odd = pltpu.bitcast(hist_ref[ci, pl.ds(1, half, stride=2), :], jnp.uint32)
        even_next = pltpu.bitcast(hist_ref[ci, pl.ds(2, half, stride=2), :], jnp.uint32)
        pa_ref[...] = (even >> 16) | (odd & hi_mask)
        pb_ref[...] = (odd >> 16) | (even_next & hi_mask)

        w = wdw_ref[ci].astype(BF16)
        taps = [jnp.broadcast_to(w[k:k + 1, :], (2 * pair_chunk, LANES)) for k in range(CONV_K)]
        b0 = jnp.broadcast_to(bdw_ref[ci], (pair_chunk, LANES))
        for r in range(tm // 2 // pair_chunk):
            base = r * pair_chunk
            acc_even, acc_odd = b0, b0
            for g0 in range(0, CONV_K, TAP_GROUP):
                prods = []
                for k in range(g0, min(g0 + TAP_GROUP, CONV_K)):
                    off = first_tap + k
                    src = pa_ref if off % 2 == 0 else pb_ref
                    words = src[pl.ds(base + off // 2, pair_chunk), :]
                    prods.append(pltpu.bitcast(words, BF16) * taps[k])
                while len(prods) > 1:
                    prods = [prods[i] + prods[i + 1] if i + 1 < len(prods) else prods[i]
                             for i in range(0, len(prods), 2)]
                gw = pltpu.bitcast(prods[0], jnp.uint32)
                acc_even = acc_even + pltpu.bitcast(gw << 16, F32)
                acc_odd = acc_odd + pltpu.bitcast(gw & hi_mask, F32)
            conv_ref[ci, pl.ds(2 * base, pair_chunk, stride=2), :] = acc_even
            conv_ref[ci, pl.ds(2 * base + 1, pair_chunk, stride=2), :] = acc_odd
        return carry
    lax.fori_loop(0, nc, lane_chunk, 0)

    inv_c = 1.0 / c
    s1 = conv_ref[0]
    for ci in range(1, nc):
        s1 = s1 + conv_ref[ci]
    mu = jnp.sum(s1, axis=1, keepdims=True) * inv_c
    s2 = jnp.zeros((tm, LANES), F32)
    for ci in range(nc):
        dlt = conv_ref[ci] - mu
        s2 = s2 + dlt * dlt
    rstd = lax.rsqrt(jnp.sum(s2, axis=1, keepdims=True) * inv_c + LN_EPS)
    for ci in range(nc):
        sl = slice(ci * LANES, (ci + 1) * LANES)
        y = (conv_ref[ci] - mu) * rstd * lng_ref[:, sl] + lnb_ref[:, sl]
        gated_ref[:, sl] = (_silu(y) * sz_ref[:, sl].astype(F32)).astype(BF16)

    y = jnp.dot(gated_ref[...], w_ref[...], preferred_element_type=F32) + bo_ref[...]
    _post_norm_residual(y, x_ref, pg_ref, o_ref)


def _conv_tail(u, sz, x2, wdw3, bdw3, ln_g, ln_b, w, b_out, pg, seq, tm=512, pair_chunk=32):
    t, d = x2.shape
    c = u.shape[1]
    nc = c // LANES
    halo_per_tile = tm // HALO_ROWS
    return pl.pallas_call(
        functools.partial(_conv_tail_kernel, tiles_per_seq=seq // tm, pair_chunk=pair_chunk),
        grid=(t // tm,),
        in_specs=[
            pl.BlockSpec((tm, c), lambda i: (i, 0)),
            pl.BlockSpec((HALO_ROWS, c), lambda i: (jnp.maximum(i * halo_per_tile - 1, 0), 0)),
            pl.BlockSpec((tm, c), lambda i: (i, 0)),
            pl.BlockSpec((tm, d), lambda i: (i, 0)),
            _const_spec(wdw3.shape),
            _const_spec(bdw3.shape),
            _const_spec((1, c)),
            _const_spec((1, c)),
            _const_spec((c, d)),
            _const_spec((1, d)),
            _const_spec((1, d)),
        ],
        out_specs=pl.BlockSpec((tm, d), lambda i: (i, 0)),
        out_shape=jax.ShapeDtypeStruct((t, d), F32),
        scratch_shapes=[pltpu.VMEM((nc, HALO_ROWS + tm + PAD_ROWS, LANES), F32),
                        pltpu.VMEM((nc, tm, LANES), F32),
                        pltpu.VMEM((tm, c), BF16),
                        pltpu.VMEM(((HALO_ROWS + tm) // 2, LANES), jnp.uint32),
                        pltpu.VMEM(((HALO_ROWS + tm) // 2, LANES), jnp.uint32)],
        compiler_params=_params(1),
        name="conv_tail",
    )(u, u, sz, x2, wdw3, bdw3, ln_g, ln_b, w, b_out, pg)


def _rope_tables(seq, head_dim):
    half = head_dim // 2
    inv_freq = 1.0 / (ROPE_THETA ** (jnp.arange(half, dtype=F32) * 2.0 / head_dim))
    ang = jnp.arange(seq, dtype=F32)[:, None] * inv_freq[None, :]
    cos, sin = jnp.cos(ang), jnp.sin(ang)
    k_tab = jnp.stack([jnp.concatenate([cos, cos], -1), jnp.concatenate([-sin, sin], -1)])
    return jnp.stack([k_tab * (head_dim ** -0.5 * LOG2_E), k_tab])


def kernel(x, pre_norm_g, post_norm_g, attn_w_in, attn_w_out, conv_w_in, conv_b_in, conv_w_dw,
           conv_b_dw, conv_ln_g, conv_ln_b, conv_w_out, conv_b_out):
    batch, seq, d = x.shape
    depth = pre_norm_g.shape[0]
    x2 = x.reshape(batch * seq, d)
    row = lambda v: v.reshape(1, -1)
    for i in range(depth):
        li = i // N_MIXERS
        pre_g, post_g = row(pre_norm_g[i]), row(post_norm_g[i])
        if i % N_MIXERS == 0:
            width = attn_w_out.shape[1]
            tab = _rope_tables(seq, width // N_HEADS)
            proj = _attn_in(x2, pre_g, attn_w_in[li], tab, seq)
            o = _moba(proj, batch, seq, width)
            x2 = _attn_out(o, proj, x2, attn_w_out[li].astype(BF16), post_g)
        else:
            c = conv_w_out.shape[1]
            nc = c // LANES
            wdw3 = jnp.pad(conv_w_dw[li], ((0, HALO_ROWS - CONV_K), (0, 0)))
            wdw3 = wdw3.reshape(HALO_ROWS, nc, LANES).transpose(1, 0, 2)
            bdw3 = conv_b_dw[li].reshape(nc, 1, LANES)
            u, sz = _conv_in(x2, pre_g, conv_w_in[li].astype(BF16), row(conv_b_in[li]))
            x2 = _conv_tail(u, sz, x2, wdw3, bdw3, row(conv_ln_g[li]), row(conv_ln_b[li]),
                            conv_w_out[li].astype(BF16), row(conv_b_out[li]), post_g, seq)
    return x2.reshape(batch, seq, d)
```

```python
import functools

import jax
import jax.numpy as jnp
from jax import lax
from jax.experimental import pallas as pl
from jax.experimental.pallas import tpu as pltpu

F32 = jnp.float32
BF16 = jnp.bfloat16

N_MIXERS = 2
N_HEADS = 16
MOBA_BLOCK = 256
MOBA_TOPK = 3
ROPE_THETA = 10000.0
CONV_K = 31
RMS_EPS = 1e-6
LN_EPS = 1e-5
NEG = -1e30
LOG2_E = 1.4426950408889634

LANES = 128
HALO_ROWS = 32
PAD_ROWS = 16
TAP_GROUP = 8
CAST_ROWS = 256
NORM_ROWS = 64
VMEM_LIMIT_BYTES = 56 * 1024 * 1024

_NT = (((1,), (1,)), ((), ()))


def _params(n_axes):
    return pltpu.CompilerParams(
        dimension_semantics=("arbitrary",) * n_axes,
        vmem_limit_bytes=VMEM_LIMIT_BYTES)


def _const_spec(shape):
    return pl.BlockSpec(shape, lambda *_: (0,) * len(shape), pipeline_mode=pl.Buffered(1))


def _rms_norm_rows(x_ref, g_ref, hn_ref):
    for r in range(x_ref.shape[0] // NORM_ROWS):
        rows = slice(r * NORM_ROWS, (r + 1) * NORM_ROWS)
        xf = x_ref[rows, :]
        ms = jnp.mean(xf * xf, axis=-1, keepdims=True)
        hn_ref[rows, :] = (xf * lax.rsqrt(ms + RMS_EPS) * g_ref[...]).astype(BF16)


def _silu(t):
    return t * jax.nn.sigmoid(t)


def _cast_rows(src_ref, dst_ref):
    for r in range(src_ref.shape[0] // CAST_ROWS):
        rows = slice(r * CAST_ROWS, (r + 1) * CAST_ROWS)
        dst_ref[rows, :] = src_ref[rows, :].astype(BF16)


def _attn_in_kernel(x_ref, g_ref, w_ref, tab_ref, o_ref, hn_ref, wb_ref, *, rope_tiles):
    j = pl.program_id(1)

    def rope_dot():
        _cast_rows(w_ref, wb_ref)
        acc = jnp.dot(hn_ref[...], wb_ref[...], preferred_element_type=F32)
        cos = tab_ref[0, 0]
        sin = tab_ref[0, 1]
        for h in range(acc.shape[1] // LANES):
            c = acc[:, h * LANES:(h + 1) * LANES]
            rot = pltpu.roll(c, LANES // 2, 1)
            o_ref[:, h * LANES:(h + 1) * LANES] = (c * cos + rot * sin).astype(BF16)

    @pl.when(j == 0)
    def _():
        _rms_norm_rows(x_ref, g_ref, hn_ref)
        rope_dot()

    @pl.when((j > 0) & (j < rope_tiles))
    def _():
        rope_dot()

    @pl.when(j >= rope_tiles)
    def _():
        _cast_rows(w_ref, wb_ref)
        o_ref[...] = jnp.dot(hn_ref[...], wb_ref[...], preferred_element_type=F32).astype(BF16)


def _attn_in(x2, g, w, tab, seq, tm=1024, tn=1024):
    t, d = x2.shape
    n = w.shape[1]
    head_dim = tab.shape[-1]
    qk_tiles = (n // 4) // tn
    tiles_per_seq = seq // tm
    return pl.pallas_call(
        functools.partial(_attn_in_kernel, rope_tiles=2 * qk_tiles),
        grid=(t // tm, n // tn),
        in_specs=[
            pl.BlockSpec((tm, d), lambda i, j: (i, 0)),
            pl.BlockSpec((1, d), lambda i, j: (0, 0)),
            pl.BlockSpec((d, tn), lambda i, j: (0, j)),
            pl.BlockSpec((1, 2, tm, head_dim),
                         lambda i, j: (jnp.minimum(j // qk_tiles, 1), 0, i % tiles_per_seq, 0)),
        ],
        out_specs=pl.BlockSpec((tm, tn), lambda i, j: (i, j)),
        out_shape=jax.ShapeDtypeStruct((t, n), BF16),
        scratch_shapes=[pltpu.VMEM((tm, d), BF16), pltpu.VMEM((d, tn), BF16)],
        compiler_params=_params(2),
        name="attn_in",
    )(x2, g, w, tab)


def _moba_kernel(q_ref, k_ref, v_ref, z_ref, o_ref, vt_ref, *bufs, heads, lookahead):
    s_len = q_ref.shape[0]
    hd = q_ref.shape[1] // heads
    s_refs, p_refs = bufs[:lookahead + 1], bufs[lookahead + 1:]
    nb = s_len // MOBA_BLOCK
    blk = MOBA_BLOCK
    ext = vt_ref.shape[1]

    for h in range(heads):
        for n in range(nb):
            vb = v_ref[n * blk:(n + 1) * blk, h * hd:(h + 1) * hd].astype(F32)
            vt_ref[h, 0:hd, n * blk:(n + 1) * blk] = vb.T.astype(BF16)
        vt_ref[h, hd:ext, :] = jnp.ones((ext - hd, s_len), BF16)

    key_pos = lax.broadcasted_iota(jnp.int32, (blk, blk), 0)
    qry_pos = lax.broadcasted_iota(jnp.int32, (blk, blk), 1)
    causal = key_pos <= qry_pos
    blk_id = lax.broadcasted_iota(jnp.int32, (nb, blk), 0)

    def scores(u):
        h, j = units[u]
        cols = slice(h * hd, (h + 1) * hd)
        q_j = q_ref[j * blk:(j + 1) * blk, cols]
        s_all = lax.dot_general(k_ref[0:(j + 1) * blk, cols], q_j, _NT,
                                preferred_element_type=F32)
        maxes, sums = [], []
        for n in range(j + 1):
            rows = slice(n * blk, (n + 1) * blk)
            s_n = s_all[rows, :]
            if n == j:
                s_n = jnp.where(causal, s_n, NEG)
            else:
                sums.append(jnp.sum(s_n, axis=0, keepdims=True))
            maxes.append(jnp.max(s_n, axis=0, keepdims=True))
            s_refs[u % len(s_refs)][rows, :] = s_n
        m_run = maxes[j]
        if j == 0:
            return None, m_run
        pad = jnp.zeros((nb - j, blk), F32)
        gate = jnp.concatenate(sums + [pad], axis=0)
        beaten = jnp.zeros((nb, blk), jnp.int32)
        for m in range(j):
            g_m = gate[m:m + 1, :]
            beats = (g_m > gate) | ((g_m == gate) & (m < blk_id))
            beaten = beaten + beats.astype(jnp.int32)
        bias = jnp.where((beaten < MOBA_TOPK) & (blk_id < j), 0.0, NEG)
        for n in range(j):
            m_run = jnp.maximum(m_run, maxes[n] + bias[n:n + 1, :])
        return bias, m_run

    units = [(h, j) for h in range(heads) for j in range(nb)]
    pending = {u: scores(u) for u in range(min(lookahead, len(units)))}
    for u, (h, j) in enumerate(units):
        if u + lookahead < len(units):
            pending[u + lookahead] = scores(u + lookahead)
        bias, m_run = pending.pop(u)
        s_ref, p_ref = s_refs[u % len(s_refs)], p_refs[u % len(p_refs)]
        for n in range(j + 1):
            rows = slice(n * blk, (n + 1) * blk)
            shift = m_run if n == j else m_run - bias[n:n + 1, :]
            p_ref[rows, :] = jnp.exp2(s_ref[rows, :] - shift).astype(BF16)
        acc = jnp.dot(vt_ref[h, :, 0:(j + 1) * blk], p_ref[0:(j + 1) * blk, :],
                      preferred_element_type=F32)
        out_t = acc[0:hd, :] * (1.0 / acc[hd:hd + 1, :])
        gate = _silu(z_ref[j * blk:(j + 1) * blk, h * hd:(h + 1) * hd].astype(F32))
        o_ref[j * blk:(j + 1) * blk, h * hd:(h + 1) * hd] = (out_t.T * gate).astype(BF16)


def _moba(proj, batch, seq, width, heads=4, lookahead=2):
    hd = width // N_HEADS
    t = proj.shape[0]
    ext = hd + 16
    groups = N_HEADS // heads
    return pl.pallas_call(
        functools.partial(_moba_kernel, heads=heads, lookahead=lookahead),
        grid=(batch, groups),
        in_specs=[
            pl.BlockSpec((seq, heads * hd), lambda b, g: (b, g)),
            pl.BlockSpec((seq, heads * hd), lambda b, g: (b, groups + g)),
            pl.BlockSpec((seq, heads * hd), lambda b, g: (b, 2 * groups + g)),
            pl.BlockSpec((seq, heads * hd), lambda b, g: (b, 3 * groups + g)),
        ],
        out_specs=pl.BlockSpec((seq, heads * hd), lambda b, g: (b, g)),
        out_shape=jax.ShapeDtypeStruct((t, width), BF16),
        scratch_shapes=([pltpu.VMEM((heads, ext, seq), BF16)]
                        + [pltpu.VMEM((seq, MOBA_BLOCK), F32)] * (lookahead + 1)
                        + [pltpu.VMEM((seq, MOBA_BLOCK), BF16)] * 2),
        compiler_params=_params(2),
        name="moba",
    )(proj, proj, proj, proj)


def _post_norm_residual(y, x_ref, pg_ref, o_ref):
    ms = jnp.mean(y * y, axis=-1, keepdims=True)
    o_ref[...] = x_ref[...] + y * lax.rsqrt(ms + RMS_EPS) * pg_ref[...]


def _attn_out_kernel(gated_ref, x_ref, w_ref, pg_ref, o_ref):
    y = jnp.dot(gated_ref[...], w_ref[...], preferred_element_type=F32)
    _post_norm_residual(y, x_ref, pg_ref, o_ref)


def _attn_out(gated, x2, w, pg, tm=512):
    t, d = x2.shape
    width = gated.shape[1]
    return pl.pallas_call(
        _attn_out_kernel,
        grid=(t // tm,),
        in_specs=[
            pl.BlockSpec((tm, width), lambda i: (i, 0)),
            pl.BlockSpec((tm, d), lambda i: (i, 0)),
            _const_spec((width, d)),
            _const_spec((1, d)),
        ],
        out_specs=pl.BlockSpec((tm, d), lambda i: (i, 0)),
        out_shape=jax.ShapeDtypeStruct((t, d), F32),
        compiler_params=_params(1),
        name="attn_out",
    )(gated, x2, w, pg)


def _conv_in_kernel(x_ref, g_ref, wa_ref, wg_ref, wz_ref, ba_ref, bg_ref, u_ref, sz_ref, hn_ref):
    def glu_dots():
        hn = hn_ref[...]
        a = jnp.dot(hn, wa_ref[...], preferred_element_type=F32) + ba_ref[...]
        g = jnp.dot(hn, wg_ref[...], preferred_element_type=F32) + bg_ref[...]
        u_ref[...] = (a * jax.nn.sigmoid(g)).astype(BF16)
        z = jnp.dot(hn, wz_ref[...], preferred_element_type=F32)
        sz_ref[...] = _silu(z).astype(BF16)

    @pl.when(pl.program_id(1) == 0)
    def _():
        _rms_norm_rows(x_ref, g_ref, hn_ref)
        glu_dots()

    @pl.when(pl.program_id(1) != 0)
    def _():
        glu_dots()


def _conv_in(x2, g, w, b_in, tm=1024, tn=512):
    t, d = x2.shape
    c = w.shape[1] // 3
    nj = c // tn
    return pl.pallas_call(
        _conv_in_kernel,
        grid=(t // tm, nj),
        in_specs=[
            pl.BlockSpec((tm, d), lambda i, j: (i, 0)),
            pl.BlockSpec((1, d), lambda i, j: (0, 0)),
            pl.BlockSpec((d, tn), lambda i, j: (0, j)),
            pl.BlockSpec((d, tn), lambda i, j: (0, nj + j)),
            pl.BlockSpec((d, tn), lambda i, j: (0, 2 * nj + j)),
            pl.BlockSpec((1, tn), lambda i, j: (0, j)),
            pl.BlockSpec((1, tn), lambda i, j: (0, nj + j)),
        ],
        out_specs=[pl.BlockSpec((tm, tn), lambda i, j: (i, j)),
                   pl.BlockSpec((tm, tn), lambda i, j: (i, j))],
        out_shape=[jax.ShapeDtypeStruct((t, c), BF16), jax.ShapeDtypeStruct((t, c), BF16)],
        scratch_shapes=[pltpu.VMEM((tm, d), BF16)],
        compiler_params=_params(2),
        name="conv_in",
    )(x2, g, w, w, w, b_in, b_in)


def _conv_tail_kernel(u_ref, halo_ref, sz_ref, x_ref, wdw_ref, bdw_ref, lng_ref, lnb_ref,
                      w_ref, bo_ref, pg_ref, o_ref, hist_ref, conv_ref, gated_ref, pa_ref, pb_ref,
                      *, tiles_per_seq, pair_chunk):
    tm, c = u_ref.shape
    nc = c // LANES

    first = (pl.program_id(0) % tiles_per_seq) == 0
    halo = jnp.where(first, 0.0, halo_ref[...].astype(F32))
    for ci in range(nc):
        hist_ref[ci, 0:HALO_ROWS, :] = halo[:, ci * LANES:(ci + 1) * LANES]
        hist_ref[ci, HALO_ROWS:HALO_ROWS + tm, :] = u_ref[:, ci * LANES:(ci + 1) * LANES].astype(F32)

    half = pa_ref.shape[0]
    hi_mask = jnp.uint32(0xFFFF0000)
    first_tap = HALO_ROWS - (CONV_K - 1)

    def lane_chunk(ci, carry):
        hist_ref[ci, HALO_ROWS + tm:, :] = jnp.zeros((hist_ref.shape[1] - HALO_ROWS - tm, LANES), F32)
        even = pltpu.bitcast(hist_ref[ci, pl.ds(0, half, stride=2), :], jnp.uint32)
        odd = pltpu.bitcast(hist_ref[ci, pl.ds(1, half, stride=2), :], jnp.uint32)
        even_next = pltpu.bitcast(hist_ref[ci, pl.ds(2, half, stride=2), :], jnp.uint32)
        pa_ref[...] = (even >> 16) | (odd & hi_mask)
        pb_ref[...] = (odd >> 16) | (even_next & hi_mask)

        w = wdw_ref[ci].astype(BF16)
        taps = [jnp.broadcast_to(w[k:k + 1, :], (2 * pair_chunk, LANES)) for k in range(CONV_K)]
        b0 = jnp.broadcast_to(bdw_ref[ci], (pair_chunk, LANES))
        for r in range(tm // 2 // pair_chunk):
            base = r * pair_chunk
            acc_even, acc_odd = b0, b0
            for g0 in range(0, CONV_K, TAP_GROUP):
                prods = []
                for k in range(g0, min(g0 + TAP_GROUP, CONV_K)):
                    off = first_tap + k
                    src = pa_ref if off % 2 == 0 else pb_ref
                    words = src[pl.ds(base + off // 2, pair_chunk), :]
                    prods.append(pltpu.bitcast(words, BF16) * taps[k])
                while len(prods) > 1:
                    prods = [prods[i] + prods[i + 1] if i + 1 < len(prods) else prods[i]
                             for i in range(0, len(prods), 2)]
                gw = pltpu.bitcast(prods[0], jnp.uint32)
                acc_even = acc_even + pltpu.bitcast(gw << 16, F32)
                acc_odd = acc_odd + pltpu.bitcast(gw & hi_mask, F32)
            conv_ref[ci, pl.ds(2 * base, pair_chunk, stride=2), :] = acc_even
            conv_ref[ci, pl.ds(2 * base + 1, pair_chunk, stride=2), :] = acc_odd
        return carry
    lax.fori_loop(0, nc, lane_chunk, 0)

    inv_c = 1.0 / c
    s1 = conv_ref[0]
    for ci in range(1, nc):
        s1 = s1 + conv_ref[ci]
    mu = jnp.sum(s1, axis=1, keepdims=True) * inv_c
    s2 = jnp.zeros((tm, LANES), F32)
    for ci in range(nc):
        dlt = conv_ref[ci] - mu
        s2 = s2 + dlt * dlt
    rstd = lax.rsqrt(jnp.sum(s2, axis=1, keepdims=True) * inv_c + LN_EPS)
    for ci in range(nc):
        sl = slice(ci * LANES, (ci + 1) * LANES)
        y = (conv_ref[ci] - mu) * rstd * lng_ref[:, sl] + lnb_ref[:, sl]
        gated_ref[:, sl] = (_silu(y) * sz_ref[:, sl].astype(F32)).astype(BF16)

    y = jnp.dot(gated_ref[...], w_ref[...], preferred_element_type=F32) + bo_ref[...]
    _post_norm_residual(y, x_ref, pg_ref, o_ref)


def _conv_tail(u, sz, x2, wdw3, bdw3, ln_g, ln_b, w, b_out, pg, seq, tm=512, pair_chunk=32):
    t, d = x2.shape
    c = u.shape[1]
    nc = c // LANES
    halo_per_tile = tm // HALO_ROWS
    return pl.pallas_call(
        functools.partial(_conv_tail_kernel, tiles_per_seq=seq // tm, pair_chunk=pair_chunk),
        grid=(t // tm,),
        in_specs=[
            pl.BlockSpec((tm, c), lambda i: (i, 0)),
            pl.BlockSpec((HALO_ROWS, c), lambda i: (jnp.maximum(i * halo_per_tile - 1, 0), 0)),
            pl.BlockSpec((tm, c), lambda i: (i, 0)),
            pl.BlockSpec((tm, d), lambda i: (i, 0)),
            _const_spec(wdw3.shape),
            _const_spec(bdw3.shape),
            _const_spec((1, c)),
            _const_spec((1, c)),
            _const_spec((c, d)),
            _const_spec((1, d)),
            _const_spec((1, d)),
        ],
        out_specs=pl.BlockSpec((tm, d), lambda i: (i, 0)),
        out_shape=jax.ShapeDtypeStruct((t, d), F32),
        scratch_shapes=[pltpu.VMEM((nc, HALO_ROWS + tm + PAD_ROWS, LANES), F32),
                        pltpu.VMEM((nc, tm, LANES), F32),
                        pltpu.VMEM((tm, c), BF16),
                        pltpu.VMEM(((HALO_ROWS + tm) // 2, LANES), jnp.uint32),
                        pltpu.VMEM(((HALO_ROWS + tm) // 2, LANES), jnp.uint32)],
        compiler_params=_params(1),
        name="conv_tail",
    )(u, u, sz, x2, wdw3, bdw3, ln_g, ln_b, w, b_out, pg)


def _rope_tables(seq, head_dim):
    half = head_dim // 2
    inv_freq = 1.0 / (ROPE_THETA ** (jnp.arange(half, dtype=F32) * 2.0 / head_dim))
    ang = jnp.arange(seq, dtype=F32)[:, None] * inv_freq[None, :]
    cos, sin = jnp.cos(ang), jnp.sin(ang)
    k_tab = jnp.stack([jnp.concatenate([cos, cos], -1), jnp.concatenate([-sin, sin], -1)])
    return jnp.stack([k_tab * (head_dim ** -0.5 * LOG2_E), k_tab])


def kernel(x, pre_norm_g, post_norm_g, attn_w_in, attn_w_out, conv_w_in, conv_b_in, conv_w_dw,
           conv_b_dw, conv_ln_g, conv_ln_b, conv_w_out, conv_b_out):
    batch, seq, d = x.shape
    depth = pre_norm_g.shape[0]
    x2 = x.reshape(batch * seq, d)
    row = lambda v: v.reshape(1, -1)
    for i in range(depth):
        li = i // N_MIXERS
        pre_g, post_g = row(pre_norm_g[i]), row(post_norm_g[i])
        if i % N_MIXERS == 0:
            width = attn_w_out.shape[1]
            tab = _rope_tables(seq, width // N_HEADS)
            proj = _attn_in(x2, pre_g, attn_w_in[li], tab, seq)
            gated = _moba(proj, batch, seq, width)
            x2 = _attn_out(gated, x2, attn_w_out[li].astype(BF16), post_g)
        else:
            c = conv_w_out.shape[1]
            nc = c // LANES
            wdw3 = jnp.pad(conv_w_dw[li], ((0, HALO_ROWS - CONV_K), (0, 0)))
            wdw3 = wdw3.reshape(HALO_ROWS, nc, LANES).transpose(1, 0, 2)
            bdw3 = conv_b_dw[li].reshape(nc, 1, LANES)
            u, sz = _conv_in(x2, pre_g, conv_w_in[li].astype(BF16), row(conv_b_in[li]))
            x2 = _conv_tail(u, sz, x2, wdw3, bdw3, row(conv_ln_g[li]), row(conv_ln_b[li]),
                            conv_w_out[li].astype(BF16), row(conv_b_out[li]), post_g, seq)
    return x2.reshape(batch, seq, d)
```

```python
import functools

import jax
import jax.numpy as jnp
from jax import lax
from jax.experimental import pallas as pl
from jax.experimental.pallas import tpu as pltpu

F32 = jnp.float32
BF16 = jnp.bfloat16

N_MIXERS = 2
N_HEADS = 16
MOBA_BLOCK = 256
MOBA_TOPK = 3
ROPE_THETA = 10000.0
CONV_K = 31
RMS_EPS = 1e-6
LN_EPS = 1e-5
NEG = -1e30
LOG2_E = 1.4426950408889634

LANES = 128
HALO_ROWS = 32
PAD_ROWS = 16
TAP_GROUP = 16
CAST_ROWS = 256
NORM_ROWS = 64
VMEM_LIMIT_BYTES = 56 * 1024 * 1024

_NT = (((1,), (1,)), ((), ()))


def _params(n_axes):
    return pltpu.CompilerParams(
        dimension_semantics=("arbitrary",) * n_axes,
        vmem_limit_bytes=VMEM_LIMIT_BYTES)


def _const_spec(shape):
    return pl.BlockSpec(shape, lambda *_: (0,) * len(shape), pipeline_mode=pl.Buffered(1))


def _rms_norm_rows(x_ref, g_ref, hn_ref):
    for r in range(x_ref.shape[0] // NORM_ROWS):
        rows = slice(r * NORM_ROWS, (r + 1) * NORM_ROWS)
        xf = x_ref[rows, :]
        ms = jnp.mean(xf * xf, axis=-1, keepdims=True)
        hn_ref[rows, :] = (xf * lax.rsqrt(ms + RMS_EPS) * g_ref[...]).astype(BF16)


def _silu(t):
    return t * jax.nn.sigmoid(t)


def _cast_rows(src_ref, dst_ref):
    for r in range(src_ref.shape[0] // CAST_ROWS):
        rows = slice(r * CAST_ROWS, (r + 1) * CAST_ROWS)
        dst_ref[rows, :] = src_ref[rows, :].astype(BF16)


def _attn_in_kernel(x_ref, g_ref, w_ref, tab_ref, o_ref, hn_ref, wb_ref, *, rope_tiles):
    j = pl.program_id(1)

    def rope_dot():
        _cast_rows(w_ref, wb_ref)
        acc = jnp.dot(hn_ref[...], wb_ref[...], preferred_element_type=F32)
        cos = tab_ref[0, 0]
        sin = tab_ref[0, 1]
        for h in range(acc.shape[1] // LANES):
            c = acc[:, h * LANES:(h + 1) * LANES]
            rot = pltpu.roll(c, LANES // 2, 1)
            o_ref[:, h * LANES:(h + 1) * LANES] = (c * cos + rot * sin).astype(BF16)

    @pl.when(j == 0)
    def _():
        _rms_norm_rows(x_ref, g_ref, hn_ref)
        rope_dot()

    @pl.when((j > 0) & (j < rope_tiles))
    def _():
        rope_dot()

    @pl.when(j >= rope_tiles)
    def _():
        _cast_rows(w_ref, wb_ref)
        o_ref[...] = jnp.dot(hn_ref[...], wb_ref[...], preferred_element_type=F32).astype(BF16)


def _attn_in(x2, g, w, tab, seq, tm=1024, tn=1024):
    t, d = x2.shape
    n = w.shape[1]
    head_dim = tab.shape[-1]
    qk_tiles = (n // 4) // tn
    tiles_per_seq = seq // tm
    return pl.pallas_call(
        functools.partial(_attn_in_kernel, rope_tiles=2 * qk_tiles),
        grid=(t // tm, n // tn),
        in_specs=[
            pl.BlockSpec((tm, d), lambda i, j: (i, 0)),
            pl.BlockSpec((1, d), lambda i, j: (0, 0)),
            pl.BlockSpec((d, tn), lambda i, j: (0, j)),
            pl.BlockSpec((1, 2, tm, head_dim),
                         lambda i, j: (jnp.minimum(j // qk_tiles, 1), 0, i % tiles_per_seq, 0)),
        ],
        out_specs=pl.BlockSpec((tm, tn), lambda i, j: (i, j)),
        out_shape=jax.ShapeDtypeStruct((t, n), BF16),
        scratch_shapes=[pltpu.VMEM((tm, d), BF16), pltpu.VMEM((d, tn), BF16)],
        compiler_params=_params(2),
        name="attn_in",
    )(x2, g, w, tab)


def _moba_kernel(q_ref, k_ref, v_ref, z_ref, o_ref, vt_ref, *bufs, heads, lookahead):
    s_len = q_ref.shape[0]
    hd = q_ref.shape[1] // heads
    s_refs, p_refs = bufs[:lookahead + 1], bufs[lookahead + 1:]
    nb = s_len // MOBA_BLOCK
    blk = MOBA_BLOCK
    ext = vt_ref.shape[1]

    for h in range(heads):
        for n in range(nb):
            vb = v_ref[n * blk:(n + 1) * blk, h * hd:(h + 1) * hd].astype(F32)
            vt_ref[h, 0:hd, n * blk:(n + 1) * blk] = vb.T.astype(BF16)
        vt_ref[h, hd:ext, :] = jnp.ones((ext - hd, s_len), BF16)

    key_pos = lax.broadcasted_iota(jnp.int32, (blk, blk), 0)
    qry_pos = lax.broadcasted_iota(jnp.int32, (blk, blk), 1)
    causal = key_pos <= qry_pos
    blk_id = lax.broadcasted_iota(jnp.int32, (nb, blk), 0)

    def scores(u):
        h, j = units[u]
        cols = slice(h * hd, (h + 1) * hd)
        q_j = q_ref[j * blk:(j + 1) * blk, cols]
        s_all = lax.dot_general(k_ref[0:(j + 1) * blk, cols], q_j, _NT,
                                preferred_element_type=F32)
        maxes, sums = [], []
        for n in range(j + 1):
            rows = slice(n * blk, (n + 1) * blk)
            s_n = s_all[rows, :]
            if n == j:
                s_n = jnp.where(causal, s_n, NEG)
            else:
                sums.append(jnp.sum(s_n, axis=0, keepdims=True))
            maxes.append(jnp.max(s_n, axis=0, keepdims=True))
            s_refs[u % len(s_refs)][rows, :] = s_n
        m_run = maxes[j]
        if j == 0:
            return None, m_run
        pad = jnp.zeros((nb - j, blk), F32)
        gate = jnp.concatenate(sums + [pad], axis=0)
        beaten = jnp.zeros((nb, blk), jnp.int32)
        for m in range(j):
            g_m = gate[m:m + 1, :]
            beats = (g_m > gate) | ((g_m == gate) & (m < blk_id))
            beaten = beaten + beats.astype(jnp.int32)
        bias = jnp.where((beaten < MOBA_TOPK) & (blk_id < j), 0.0, NEG)
        for n in range(j):
            m_run = jnp.maximum(m_run, maxes[n] + bias[n:n + 1, :])
        return bias, m_run

    units = [(h, j) for h in range(heads) for j in range(nb)]
    pending = {u: scores(u) for u in range(min(lookahead, len(units)))}
    for u, (h, j) in enumerate(units):
        if u + lookahead < len(units):
            pending[u + lookahead] = scores(u + lookahead)
        bias, m_run = pending.pop(u)
        s_ref, p_ref = s_refs[u % len(s_refs)], p_refs[u % len(p_refs)]
        for n in range(j + 1):
            rows = slice(n * blk, (n + 1) * blk)
            shift = m_run if n == j else m_run - bias[n:n + 1, :]
            p_ref[rows, :] = jnp.exp2(s_ref[rows, :] - shift).astype(BF16)
        acc = jnp.dot(vt_ref[h, :, 0:(j + 1) * blk], p_ref[0:(j + 1) * blk, :],
                      preferred_element_type=F32)
        out_t = acc[0:hd, :] * (1.0 / acc[hd:hd + 1, :])
        gate = _silu(z_ref[j * blk:(j + 1) * blk, h * hd:(h + 1) * hd].astype(F32))
        o_ref[j * blk:(j + 1) * blk, h * hd:(h + 1) * hd] = (out_t.T * gate).astype(BF16)


def _moba(proj, batch, seq, width, heads=4, lookahead=3):
    hd = width // N_HEADS
    t = proj.shape[0]
    ext = hd + 16
    groups = N_HEADS // heads
    return pl.pallas_call(
        functools.partial(_moba_kernel, heads=heads, lookahead=lookahead),
        grid=(batch, groups),
        in_specs=[
            pl.BlockSpec((seq, heads * hd), lambda b, g: (b, g)),
            pl.BlockSpec((seq, heads * hd), lambda b, g: (b, groups + g)),
            pl.BlockSpec((seq, heads * hd), lambda b, g: (b, 2 * groups + g)),
            pl.BlockSpec((seq, heads * hd), lambda b, g: (b, 3 * groups + g)),
        ],
        out_specs=pl.BlockSpec((seq, heads * hd), lambda b, g: (b, g)),
        out_shape=jax.ShapeDtypeStruct((t, width), BF16),
        scratch_shapes=([pltpu.VMEM((heads, ext, seq), BF16)]
                        + [pltpu.VMEM((seq, MOBA_BLOCK), F32)] * (lookahead + 1)
                        + [pltpu.VMEM((seq, MOBA_BLOCK), BF16)] * 2),
        compiler_params=_params(2),
        name="moba",
    )(proj, proj, proj, proj)


def _post_norm_residual(y, x_ref, pg_ref, o_ref):
    ms = jnp.mean(y * y, axis=-1, keepdims=True)
    o_ref[...] = x_ref[...] + y * lax.rsqrt(ms + RMS_EPS) * pg_ref[...]


def _attn_out_kernel(gated_ref, x_ref, w_ref, pg_ref, o_ref):
    y = jnp.dot(gated_ref[...], w_ref[...], preferred_element_type=F32)
    _post_norm_residual(y, x_ref, pg_ref, o_ref)


def _attn_out(gated, x2, w, pg, tm=512):
    t, d = x2.shape
    width = gated.shape[1]
    return pl.pallas_call(
        _attn_out_kernel,
        grid=(t // tm,),
        in_specs=[
            pl.BlockSpec((tm, width), lambda i: (i, 0)),
            pl.BlockSpec((tm, d), lambda i: (i, 0)),
            _const_spec((width, d)),
            _const_spec((1, d)),
        ],
        out_specs=pl.BlockSpec((tm, d), lambda i: (i, 0)),
        out_shape=jax.ShapeDtypeStruct((t, d), F32),
        compiler_params=_params(1),
        name="attn_out",
    )(gated, x2, w, pg)


def _conv_in_kernel(x_ref, g_ref, wa_ref, wg_ref, wz_ref, ba_ref, bg_ref, u_ref, sz_ref, hn_ref):
    def glu_dots():
        hn = hn_ref[...]
        a = jnp.dot(hn, wa_ref[...], preferred_element_type=F32) + ba_ref[...]
        g = jnp.dot(hn, wg_ref[...], preferred_element_type=F32) + bg_ref[...]
        u_ref[...] = (a * jax.nn.sigmoid(g)).astype(BF16)
        z = jnp.dot(hn, wz_ref[...], preferred_element_type=F32)
        sz_ref[...] = _silu(z).astype(BF16)

    @pl.when(pl.program_id(1) == 0)
    def _():
        _rms_norm_rows(x_ref, g_ref, hn_ref)
        glu_dots()

    @pl.when(pl.program_id(1) != 0)
    def _():
        glu_dots()


def _conv_in(x2, g, w, b_in, tm=1024, tn=512):
    t, d = x2.shape
    c = w.shape[1] // 3
    nj = c // tn
    return pl.pallas_call(
        _conv_in_kernel,
        grid=(t // tm, nj),
        in_specs=[
            pl.BlockSpec((tm, d), lambda i, j: (i, 0)),
            pl.BlockSpec((1, d), lambda i, j: (0, 0)),
            pl.BlockSpec((d, tn), lambda i, j: (0, j)),
            pl.BlockSpec((d, tn), lambda i, j: (0, nj + j)),
            pl.BlockSpec((d, tn), lambda i, j: (0, 2 * nj + j)),
            pl.BlockSpec((1, tn), lambda i, j: (0, j)),
            pl.BlockSpec((1, tn), lambda i, j: (0, nj + j)),
        ],
        out_specs=[pl.BlockSpec((tm, tn), lambda i, j: (i, j)),
                   pl.BlockSpec((tm, tn), lambda i, j: (i, j))],
        out_shape=[jax.ShapeDtypeStruct((t, c), BF16), jax.ShapeDtypeStruct((t, c), BF16)],
        scratch_shapes=[pltpu.VMEM((tm, d), BF16)],
        compiler_params=_params(2),
        name="conv_in",
    )(x2, g, w, w, w, b_in, b_in)


def _conv_tail_kernel(u_ref, halo_ref, sz_ref, x_ref, wdw_ref, bdw_ref, lng_ref, lnb_ref,
                      w_ref, bo_ref, pg_ref, o_ref, hist_ref, conv_ref, gated_ref, pa_ref, pb_ref,
                      *, tiles_per_seq, pair_chunk):
    tm, c = u_ref.shape
    nc = c // LANES

    first = (pl.program_id(0) % tiles_per_seq) == 0
    halo = jnp.where(first, 0.0, halo_ref[...].astype(F32))
    for ci in range(nc):
        hist_ref[ci, 0:HALO_ROWS, :] = halo[:, ci * LANES:(ci + 1) * LANES]
        hist_ref[ci, HALO_ROWS:HALO_ROWS + tm, :] = u_ref[:, ci * LANES:(ci + 1) * LANES].astype(F32)

    half = pa_ref.shape[0]
    hi_mask = jnp.uint32(0xFFFF0000)
    first_tap = HALO_ROWS - (CONV_K - 1)

    def lane_chunk(ci, carry):
        hist_ref[ci, HALO_ROWS + tm:, :] = jnp.zeros((hist_ref.shape[1] - HALO_ROWS - tm, LANES), F32)
        even = pltpu.bitcast(hist_ref[ci, pl.ds(0, half, stride=2), :], jnp.uint32)
        odd = pltpu.bitcast(hist_ref[ci, pl.ds(1, half, stride=2), :], jnp.uint32)
        even_next = pltpu.bitcast(hist_ref[ci, pl.ds(2, half, stride=2), :], jnp.uint32)
        pa_ref[...] = (even >> 16) | (odd & hi_mask)
        pb_ref[...] = (odd >> 16) | (even_next & hi_mask)

        w = wdw_ref[ci].astype(BF16)
        taps = [jnp.broadcast_to(w[k:k + 1, :], (2 * pair_chunk, LANES)) for k in range(CONV_K)]
        b0 = jnp.broadcast_to(bdw_ref[ci], (pair_chunk, LANES))
        for r in range(tm // 2 // pair_chunk):
            base = r * pair_chunk
            acc_even, acc_odd = b0, b0
            for g0 in range(0, CONV_K, TAP_GROUP):
                prods = []
                for k in range(g0, min(g0 + TAP_GROUP, CONV_K)):
                    off = first_tap + k
                    src = pa_ref if off % 2 == 0 else pb_ref
                    words = src[pl.ds(base + off // 2, pair_chunk), :]
                    prods.append(pltpu.bitcast(words, BF16) * taps[k])
                while len(prods) > 1:
                    prods = [prods[i] + prods[i + 1] if i + 1 < len(prods) else prods[i]
                             for i in range(0, len(prods), 2)]
                gw = pltpu.bitcast(prods[0], jnp.uint32)
                acc_even = acc_even + pltpu.bitcast(gw << 16, F32)
                acc_odd = acc_odd + pltpu.bitcast(gw & hi_mask, F32)
            conv_ref[ci, pl.ds(2 * base, pair_chunk, stride=2), :] = acc_even
            conv_ref[ci, pl.ds(2 * base + 1, pair_chunk, stride=2), :] = acc_odd
        return carry
    lax.fori_loop(0, nc, lane_chunk, 0)

    inv_c = 1.0 / c
    s1 = conv_ref[0]
    for ci in range(1, nc):
        s1 = s1 + conv_ref[ci]
    mu = jnp.sum(s1, axis=1, keepdims=True) * inv_c
    s2 = jnp.zeros((tm, LANES), F32)
    for ci in range(nc):
        dlt = conv_ref[ci] - mu
        s2 = s2 + dlt * dlt
    rstd = lax.rsqrt(jnp.sum(s2, axis=1, keepdims=True) * inv_c + LN_EPS)
    for ci in range(nc):
        sl = slice(ci * LANES, (ci + 1) * LANES)
        y = (conv_ref[ci] - mu) * rstd * lng_ref[:, sl] + lnb_ref[:, sl]
        gated_ref[:, sl] = (_silu(y) * sz_ref[:, sl].astype(F32)).astype(BF16)

    y = jnp.dot(gated_ref[...], w_ref[...], preferred_element_type=F32) + bo_ref[...]
    _post_norm_residual(y, x_ref, pg_ref, o_ref)


def _conv_tail(u, sz, x2, wdw3, bdw3, ln_g, ln_b, w, b_out, pg, seq, tm=512, pair_chunk=32):
    t, d = x2.shape
    c = u.shape[1]
    nc = c // LANES
    halo_per_tile = tm // HALO_ROWS
    return pl.pallas_call(
        functools.partial(_conv_tail_kernel, tiles_per_seq=seq // tm, pair_chunk=pair_chunk),
        grid=(t // tm,),
        in_specs=[
            pl.BlockSpec((tm, c), lambda i: (i, 0)),
            pl.BlockSpec((HALO_ROWS, c), lambda i: (jnp.maximum(i * halo_per_tile - 1, 0), 0)),
            pl.BlockSpec((tm, c), lambda i: (i, 0)),
            pl.BlockSpec((tm, d), lambda i: (i, 0)),
            _const_spec(wdw3.shape),
            _const_spec(bdw3.shape),
            _const_spec((1, c)),
            _const_spec((1, c)),
            _const_spec((c, d)),
            _const_spec((1, d)),
            _const_spec((1, d)),
        ],
        out_specs=pl.BlockSpec((tm, d), lambda i: (i, 0)),
        out_shape=jax.ShapeDtypeStruct((t, d), F32),
        scratch_shapes=[pltpu.VMEM((nc, HALO_ROWS + tm + PAD_ROWS, LANES), F32),
                        pltpu.VMEM((nc, tm, LANES), F32),
                        pltpu.VMEM((tm, c), BF16),
                        pltpu.VMEM(((HALO_ROWS + tm) // 2, LANES), jnp.uint32),
                        pltpu.VMEM(((HALO_ROWS + tm) // 2, LANES), jnp.uint32)],
        compiler_params=_params(1),
        name="conv_tail",
    )(u, u, sz, x2, wdw3, bdw3, ln_g, ln_b, w, b_out, pg)


def _rope_tables(seq, head_dim):
    half = head_dim // 2
    inv_freq = 1.0 / (ROPE_THETA ** (jnp.arange(half, dtype=F32) * 2.0 / head_dim))
    ang = jnp.arange(seq, dtype=F32)[:, None] * inv_freq[None, :]
    cos, sin = jnp.cos(ang), jnp.sin(ang)
    k_tab = jnp.stack([jnp.concatenate([cos, cos], -1), jnp.concatenate([-sin, sin], -1)])
    return jnp.stack([k_tab * (head_dim ** -0.5 * LOG2_E), k_tab])


def kernel(x, pre_norm_g, post_norm_g, attn_w_in, attn_w_out, conv_w_in, conv_b_in, conv_w_dw,
           conv_b_dw, conv_ln_g, conv_ln_b, conv_w_out, conv_b_out):
    batch, seq, d = x.shape
    depth = pre_norm_g.shape[0]
    x2 = x.reshape(batch * seq, d)
    row = lambda v: v.reshape(1, -1)
    for i in range(depth):
        li = i // N_MIXERS
        pre_g, post_g = row(pre_norm_g[i]), row(post_norm_g[i])
        if i % N_MIXERS == 0:
            width = attn_w_out.shape[1]
            tab = _rope_tables(seq, width // N_HEADS)
            proj = _attn_in(x2, pre_g, attn_w_in[li], tab, seq)
            gated = _moba(proj, batch, seq, width)
            x2 = _attn_out(gated, x2, attn_w_out[li].astype(BF16), post_g)
        else:
            c = conv_w_out.shape[1]
            nc = c // LANES
            wdw3 = jnp.pad(conv_w_dw[li], ((0, HALO_ROWS - CONV_K), (0, 0)))
            wdw3 = wdw3.reshape(HALO_ROWS, nc, LANES).transpose(1, 0, 2)
            bdw3 = conv_b_dw[li].reshape(nc, 1, LANES)
            u, sz = _conv_in(x2, pre_g, conv_w_in[li].astype(BF16), row(conv_b_in[li]))
            x2 = _conv_tail(u, sz, x2, wdw3, bdw3, row(conv_ln_g[li]), row(conv_ln_b[li]),
                            conv_w_out[li].astype(BF16), row(conv_b_out[li]), post_g, seq)
    return x2.reshape(batch, seq, d)
```

```python
import functools

import jax
import jax.numpy as jnp
from jax import lax
from jax.experimental import pallas as pl
from jax.experimental.pallas import tpu as pltpu

F32 = jnp.float32
BF16 = jnp.bfloat16

N_MIXERS = 2
N_HEADS = 16
MOBA_BLOCK = 256
MOBA_TOPK = 3
ROPE_THETA = 10000.0
CONV_K = 31
RMS_EPS = 1e-6
LN_EPS = 1e-5
NEG = -1e30
LOG2_E = 1.4426950408889634

LANES = 128
HALO_ROWS = 32
PAD_ROWS = 16
TAP_GROUP = 16
CAST_ROWS = 256
NORM_ROWS = 64
VMEM_LIMIT_BYTES = 56 * 1024 * 1024

_NT = (((1,), (1,)), ((), ()))


def _params(n_axes):
    return pltpu.CompilerParams(
        dimension_semantics=("arbitrary",) * n_axes,
        vmem_limit_bytes=VMEM_LIMIT_BYTES)


def _const_spec(shape):
    return pl.BlockSpec(shape, lambda *_: (0,) * len(shape), pipeline_mode=pl.Buffered(1))


def _rms_norm_rows(x_ref, g_ref, hn_ref):
    for r in range(x_ref.shape[0] // NORM_ROWS):
        rows = slice(r * NORM_ROWS, (r + 1) * NORM_ROWS)
        xf = x_ref[rows, :]
        ms = jnp.mean(xf * xf, axis=-1, keepdims=True)
        hn_ref[rows, :] = (xf * lax.rsqrt(ms + RMS_EPS) * g_ref[...]).astype(BF16)


def _silu(t):
    return t * jax.nn.sigmoid(t)


def _cast_rows(src_ref, dst_ref):
    for r in range(src_ref.shape[0] // CAST_ROWS):
        rows = slice(r * CAST_ROWS, (r + 1) * CAST_ROWS)
        dst_ref[rows, :] = src_ref[rows, :].astype(BF16)


def _attn_in_kernel(x_ref, g_ref, w_ref, tab_ref, o_ref, hn_ref, wb_ref, *, rope_tiles):
    j = pl.program_id(1)

    def rope_dot():
        _cast_rows(w_ref, wb_ref)
        acc = jnp.dot(hn_ref[...], wb_ref[...], preferred_element_type=F32)
        cos = tab_ref[0, 0]
        sin = tab_ref[0, 1]
        for h in range(acc.shape[1] // LANES):
            c = acc[:, h * LANES:(h + 1) * LANES]
            rot = pltpu.roll(c, LANES // 2, 1)
            o_ref[:, h * LANES:(h + 1) * LANES] = (c * cos + rot * sin).astype(BF16)

    @pl.when(j == 0)
    def _():
        _rms_norm_rows(x_ref, g_ref, hn_ref)
        rope_dot()

    @pl.when((j > 0) & (j < rope_tiles))
    def _():
        rope_dot()

    @pl.when(j >= rope_tiles)
    def _():
        _cast_rows(w_ref, wb_ref)
        o_ref[...] = jnp.dot(hn_ref[...], wb_ref[...], preferred_element_type=F32).astype(BF16)


def _attn_in(x2, g, w, tab, seq, tm=1024, tn=1024):
    t, d = x2.shape
    n = w.shape[1]
    head_dim = tab.shape[-1]
    qk_tiles = (n // 4) // tn
    tiles_per_seq = seq // tm
    return pl.pallas_call(
        functools.partial(_attn_in_kernel, rope_tiles=2 * qk_tiles),
        grid=(t // tm, n // tn),
        in_specs=[
            pl.BlockSpec((tm, d), lambda i, j: (i, 0)),
            pl.BlockSpec((1, d), lambda i, j: (0, 0)),
            pl.BlockSpec((d, tn), lambda i, j: (0, j)),
            pl.BlockSpec((1, 2, tm, head_dim),
                         lambda i, j: (jnp.minimum(j // qk_tiles, 1), 0, i % tiles_per_seq, 0)),
        ],
        out_specs=pl.BlockSpec((tm, tn), lambda i, j: (i, j)),
        out_shape=jax.ShapeDtypeStruct((t, n), BF16),
        scratch_shapes=[pltpu.VMEM((tm, d), BF16), pltpu.VMEM((d, tn), BF16)],
        compiler_params=_params(2),
        name="attn_in",
    )(x2, g, w, tab)


def _moba_kernel(q_ref, k_ref, v_ref, z_ref, o_ref, vt_ref, *bufs, heads, lookahead):
    s_len = q_ref.shape[0]
    hd = q_ref.shape[1] // heads
    s_refs, p_refs = bufs[:lookahead + 1], bufs[lookahead + 1:]
    nb = s_len // MOBA_BLOCK
    blk = MOBA_BLOCK
    ext = vt_ref.shape[1]

    for h in range(heads):
        for n in range(nb):
            vb = v_ref[n * blk:(n + 1) * blk, h * hd:(h + 1) * hd].astype(F32)
            vt_ref[h, 0:hd, n * blk:(n + 1) * blk] = vb.T.astype(BF16)
        vt_ref[h, hd:ext, :] = jnp.ones((ext - hd, s_len), BF16)

    key_pos = lax.broadcasted_iota(jnp.int32, (blk, blk), 0)
    qry_pos = lax.broadcasted_iota(jnp.int32, (blk, blk), 1)
    causal = key_pos <= qry_pos
    blk_id = lax.broadcasted_iota(jnp.int32, (nb, blk), 0)

    def scores(u):
        h, j = units[u]
        cols = slice(h * hd, (h + 1) * hd)
        q_j = q_ref[j * blk:(j + 1) * blk, cols]
        s_all = lax.dot_general(k_ref[0:(j + 1) * blk, cols], q_j, _NT,
                                preferred_element_type=F32)
        maxes, sums = [], []
        for n in range(j + 1):
            rows = slice(n * blk, (n + 1) * blk)
            s_n = s_all[rows, :]
            if n == j:
                s_n = jnp.where(causal, s_n, NEG)
            else:
                sums.append(jnp.sum(s_n, axis=0, keepdims=True))
            maxes.append(jnp.max(s_n, axis=0, keepdims=True))
            s_refs[u % len(s_refs)][rows, :] = s_n
        m_run = maxes[j]
        if j == 0:
            return None, m_run
        pad = jnp.zeros((nb - j, blk), F32)
        gate = jnp.concatenate(sums + [pad], axis=0)
        beaten = jnp.zeros((nb, blk), jnp.int32)
        for m in range(j):
            g_m = gate[m:m + 1, :]
            beats = (g_m > gate) | ((g_m == gate) & (m < blk_id))
            beaten = beaten + beats.astype(jnp.int32)
        bias = jnp.where((beaten < MOBA_TOPK) & (blk_id < j), 0.0, NEG)
        for n in range(j):
            m_run = jnp.maximum(m_run, maxes[n] + bias[n:n + 1, :])
        return bias, m_run

    units = [(h, j) for h in range(heads) for j in range(nb)]
    pending = {u: scores(u) for u in range(min(lookahead, len(units)))}
    for u, (h, j) in enumerate(units):
        if u + lookahead < len(units):
            pending[u + lookahead] = scores(u + lookahead)
        bias, m_run = pending.pop(u)
        s_ref, p_ref = s_refs[u % len(s_refs)], p_refs[u % len(p_refs)]
        for n in range(j + 1):
            rows = slice(n * blk, (n + 1) * blk)
            shift = m_run if n == j else m_run - bias[n:n + 1, :]
            p_ref[rows, :] = jnp.exp2(s_ref[rows, :] - shift).astype(BF16)
        acc = jnp.dot(vt_ref[h, :, 0:(j + 1) * blk], p_ref[0:(j + 1) * blk, :],
                      preferred_element_type=F32)
        out_t = acc[0:hd, :] * (1.0 / acc[hd:hd + 1, :])
        gate = _silu(z_ref[j * blk:(j + 1) * blk, h * hd:(h + 1) * hd].astype(F32))
        o_ref[j * blk:(j + 1) * blk, h * hd:(h + 1) * hd] = (out_t.T * gate).astype(BF16)


def _moba(proj, batch, seq, width, heads=4, lookahead=2):
    hd = width // N_HEADS
    t = proj.shape[0]
    ext = hd + 16
    groups = N_HEADS // heads
    return pl.pallas_call(
        functools.partial(_moba_kernel, heads=heads, lookahead=lookahead),
        grid=(batch, groups),
        in_specs=[
            pl.BlockSpec((seq, heads * hd), lambda b, g: (b, g)),
            pl.BlockSpec((seq, heads * hd), lambda b, g: (b, groups + g)),
            pl.BlockSpec((seq, heads * hd), lambda b, g: (b, 2 * groups + g)),
            pl.BlockSpec((seq, heads * hd), lambda b, g: (b, 3 * groups + g)),
        ],
        out_specs=pl.BlockSpec((seq, heads * hd), lambda b, g: (b, g)),
        out_shape=jax.ShapeDtypeStruct((t, width), BF16),
        scratch_shapes=([pltpu.VMEM((heads, ext, seq), BF16)]
                        + [pltpu.VMEM((seq, MOBA_BLOCK), F32)] * (lookahead + 1)
                        + [pltpu.VMEM((seq, MOBA_BLOCK), BF16)] * 2),
        compiler_params=_params(2),
        name="moba",
    )(proj, proj, proj, proj)


def _post_norm_residual(y, x_ref, pg_ref, o_ref):
    ms = jnp.mean(y * y, axis=-1, keepdims=True)
    o_ref[...] = x_ref[...] + y * lax.rsqrt(ms + RMS_EPS) * pg_ref[...]


def _attn_out_kernel(gated_ref, x_ref, w_ref, pg_ref, o_ref):
    y = jnp.dot(gated_ref[...], w_ref[...], preferred_element_type=F32)
    _post_norm_residual(y, x_ref, pg_ref, o_ref)


def _attn_out(gated, x2, w, pg, tm=512):
    t, d = x2.shape
    width = gated.shape[1]
    return pl.pallas_call(
        _attn_out_kernel,
        grid=(t // tm,),
        in_specs=[
            pl.BlockSpec((tm, width), lambda i: (i, 0)),
            pl.BlockSpec((tm, d), lambda i: (i, 0)),
            _const_spec((width, d)),
            _const_spec((1, d)),
        ],
        out_specs=pl.BlockSpec((tm, d), lambda i: (i, 0)),
        out_shape=jax.ShapeDtypeStruct((t, d), F32),
        compiler_params=_params(1),
        name="attn_out",
    )(gated, x2, w, pg)


def _conv_in_kernel(x_ref, g_ref, wa_ref, wg_ref, wz_ref, ba_ref, bg_ref, u_ref, sz_ref, hn_ref):
    def glu_dots():
        hn = hn_ref[...]
        a = jnp.dot(hn, wa_ref[...], preferred_element_type=F32) + ba_ref[...]
        g = jnp.dot(hn, wg_ref[...], preferred_element_type=F32) + bg_ref[...]
        u_ref[...] = (a * jax.nn.sigmoid(g)).astype(BF16)
        z = jnp.dot(hn, wz_ref[...], preferred_element_type=F32)
        sz_ref[...] = _silu(z).astype(BF16)

    @pl.when(pl.program_id(1) == 0)
    def _():
        _rms_norm_rows(x_ref, g_ref, hn_ref)
        glu_dots()

    @pl.when(pl.program_id(1) != 0)
    def _():
        glu_dots()


def _conv_in(x2, g, w, b_in, tm=1024, tn=512):
    t, d = x2.shape
    c = w.shape[1] // 3
    nj = c // tn
    return pl.pallas_call(
        _conv_in_kernel,
        grid=(t // tm, nj),
        in_specs=[
            pl.BlockSpec((tm, d), lambda i, j: (i, 0)),
            pl.BlockSpec((1, d), lambda i, j: (0, 0)),
            pl.BlockSpec((d, tn), lambda i, j: (0, j)),
            pl.BlockSpec((d, tn), lambda i, j: (0, nj + j)),
            pl.BlockSpec((d, tn), lambda i, j: (0, 2 * nj + j)),
            pl.BlockSpec((1, tn), lambda i, j: (0, j)),
            pl.BlockSpec((1, tn), lambda i, j: (0, nj + j)),
        ],
        out_specs=[pl.BlockSpec((tm, tn), lambda i, j: (i, j)),
                   pl.BlockSpec((tm, tn), lambda i, j: (i, j))],
        out_shape=[jax.ShapeDtypeStruct((t, c), BF16), jax.ShapeDtypeStruct((t, c), BF16)],
        scratch_shapes=[pltpu.VMEM((tm, d), BF16)],
        compiler_params=_params(2),
        name="conv_in",
    )(x2, g, w, w, w, b_in, b_in)


def _conv_tail_kernel(u_ref, halo_ref, sz_ref, x_ref, wdw_ref, bdw_ref, lng_ref, lnb_ref,
                      w_ref, bo_ref, pg_ref, o_ref, hist_ref, conv_ref, gated_ref, pa_ref, pb_ref,
                      *, tiles_per_seq, pair_chunk):
    tm, c = u_ref.shape
    nc = c // LANES

    first = (pl.program_id(0) % tiles_per_seq) == 0
    halo = jnp.where(first, 0.0, halo_ref[...].astype(F32))
    for ci in range(nc):
        hist_ref[ci, 0:HALO_ROWS, :] = halo[:, ci * LANES:(ci + 1) * LANES]
        hist_ref[ci, HALO_ROWS:HALO_ROWS + tm, :] = u_ref[:, ci * LANES:(ci + 1) * LANES].astype(F32)

    half = pa_ref.shape[0]
    hi_mask = jnp.uint32(0xFFFF0000)
    first_tap = HALO_ROWS - (CONV_K - 1)

    def lane_chunk(ci, carry):
        hist_ref[ci, HALO_ROWS + tm:, :] = jnp.zeros((hist_ref.shape[1] - HALO_ROWS - tm, LANES), F32)
        even = pltpu.bitcast(hist_ref[ci, pl.ds(0, half, stride=2), :], jnp.uint32)
        odd = pltpu.bitcast(hist_ref[ci, pl.ds(1, half, stride=2), :], jnp.uint32)
        even_next = pltpu.bitcast(hist_ref[ci, pl.ds(2, half, stride=2), :], jnp.uint32)
        pa_ref[...] = (even >> 16) | (odd & hi_mask)
        pb_ref[...] = (odd >> 16) | (even_next & hi_mask)

        w = wdw_ref[ci].astype(BF16)
        taps = [jnp.broadcast_to(w[k:k + 1, :], (2 * pair_chunk, LANES)) for k in range(CONV_K)]
        b0 = jnp.broadcast_to(bdw_ref[ci], (pair_chunk, LANES))
        for r in range(tm // 2 // pair_chunk):
            base = r * pair_chunk
            acc_even, acc_odd = b0, b0
            for g0 in range(0, CONV_K, TAP_GROUP):
                prods = []
                for k in range(g0, min(g0 + TAP_GROUP, CONV_K)):
                    off = first_tap + k
                    src = pa_ref if off % 2 == 0 else pb_ref
                    words = src[pl.ds(base + off // 2, pair_chunk), :]
                    prods.append(pltpu.bitcast(words, BF16) * taps[k])
                while len(prods) > 1:
                    prods = [prods[i] + prods[i + 1] if i + 1 < len(prods) else prods[i]
                             for i in range(0, len(prods), 2)]
                gw = pltpu.bitcast(prods[0], jnp.uint32)
                acc_even = acc_even + pltpu.bitcast(gw << 16, F32)
                acc_odd = acc_odd + pltpu.bitcast(gw & hi_mask, F32)
            conv_ref[ci, pl.ds(2 * base, pair_chunk, stride=2), :] = acc_even
            conv_ref[ci, pl.ds(2 * base + 1, pair_chunk, stride=2), :] = acc_odd
        return carry
    lax.fori_loop(0, nc, lane_chunk, 0)

    inv_c = 1.0 / c
    s1 = conv_ref[0]
    for ci in range(1, nc):
        s1 = s1 + conv_ref[ci]
    mu = jnp.sum(s1, axis=1, keepdims=True) * inv_c
    s2 = jnp.zeros((tm, LANES), F32)
    for ci in range(nc):
        dlt = conv_ref[ci] - mu
        s2 = s2 + dlt * dlt
    rstd = lax.rsqrt(jnp.sum(s2, axis=1, keepdims=True) * inv_c + LN_EPS)
    for ci in range(nc):
        sl = slice(ci * LANES, (ci + 1) * LANES)
        y = (conv_ref[ci] - mu) * rstd * lng_ref[:, sl] + lnb_ref[:, sl]
        gated_ref[:, sl] = (_silu(y) * sz_ref[:, sl].astype(F32)).astype(BF16)

    y = jnp.dot(gated_ref[...], w_ref[...], preferred_element_type=F32) + bo_ref[...]
    _post_norm_residual(y, x_ref, pg_ref, o_ref)


def _conv_tail(u, sz, x2, wdw3, bdw3, ln_g, ln_b, w, b_out, pg, seq, tm=512, pair_chunk=32):
    t, d = x2.shape
    c = u.shape[1]
    nc = c // LANES
    halo_per_tile = tm // HALO_ROWS
    return pl.pallas_call(
        functools.partial(_conv_tail_kernel, tiles_per_seq=seq // tm, pair_chunk=pair_chunk),
        grid=(t // tm,),
        in_specs=[
            pl.BlockSpec((tm, c), lambda i: (i, 0)),
            pl.BlockSpec((HALO_ROWS, c), lambda i: (jnp.maximum(i * halo_per_tile - 1, 0), 0)),
            pl.BlockSpec((tm, c), lambda i: (i, 0)),
            pl.BlockSpec((tm, d), lambda i: (i, 0)),
            _const_spec(wdw3.shape),
            _const_spec(bdw3.shape),
            _const_spec((1, c)),
            _const_spec((1, c)),
            _const_spec((c, d)),
            _const_spec((1, d)),
            _const_spec((1, d)),
        ],
        out_specs=pl.BlockSpec((tm, d), lambda i: (i, 0)),
        out_shape=jax.ShapeDtypeStruct((t, d), F32),
        scratch_shapes=[pltpu.VMEM((nc, HALO_ROWS + tm + PAD_ROWS, LANES), F32),
                        pltpu.VMEM((nc, tm, LANES), F32),
                        pltpu.VMEM((tm, c), BF16),
                        pltpu.VMEM(((HALO_ROWS + tm) // 2, LANES), jnp.uint32),
                        pltpu.VMEM(((HALO_ROWS + tm) // 2, LANES), jnp.uint32)],
        compiler_params=_params(1),
        name="conv_tail",
    )(u, u, sz, x2, wdw3, bdw3, ln_g, ln_b, w, b_out, pg)


def _rope_tables(seq, head_dim):
    half = head_dim // 2
    inv_freq = 1.0 / (ROPE_THETA ** (jnp.arange(half, dtype=F32) * 2.0 / head_dim))
    ang = jnp.arange(seq, dtype=F32)[:, None] * inv_freq[None, :]
    cos, sin = jnp.cos(ang), jnp.sin(ang)
    k_tab = jnp.stack([jnp.concatenate([cos, cos], -1), jnp.concatenate([-sin, sin], -1)])
    return jnp.stack([k_tab * (head_dim ** -0.5 * LOG2_E), k_tab])


def kernel(x, pre_norm_g, post_norm_g, attn_w_in, attn_w_out, conv_w_in, conv_b_in, conv_w_dw,
           conv_b_dw, conv_ln_g, conv_ln_b, conv_w_out, conv_b_out):
    batch, seq, d = x.shape
    depth = pre_norm_g.shape[0]
    x2 = x.reshape(batch * seq, d)
    row = lambda v: v.reshape(1, -1)
    for i in range(depth):
        li = i // N_MIXERS
        pre_g, post_g = row(pre_norm_g[i]), row(post_norm_g[i])
        if i % N_MIXERS == 0:
            width = attn_w_out.shape[1]
            tab = _rope_tables(seq, width // N_HEADS)
            proj = _attn_in(x2, pre_g, attn_w_in[li], tab, seq)
            gated = _moba(proj, batch, seq, width)
            x2 = _attn_out(gated, x2, attn_w_out[li].astype(BF16), post_g)
        else:
            c = conv_w_out.shape[1]
            nc = c // LANES
            wdw3 = jnp.pad(conv_w_dw[li], ((0, HALO_ROWS - CONV_K), (0, 0)))
            wdw3 = wdw3.reshape(HALO_ROWS, nc, LANES).transpose(1, 0, 2)
            bdw3 = conv_b_dw[li].reshape(nc, 1, LANES)
            u, sz = _conv_in(x2, pre_g, conv_w_in[li].astype(BF16), row(conv_b_in[li]))
            x2 = _conv_tail(u, sz, x2, wdw3, bdw3, row(conv_ln_g[li]), row(conv_ln_b[li]),
                            conv_w_out[li].astype(BF16), row(conv_b_out[li]), post_g, seq)
    return x2.reshape(batch, seq, d)
```
